```python
import jax, jax.numpy as jnp
from jax import lax
import numpy as np

D_MODEL = 1024
BATCH = 32
SEQ = 256
DEPTH = 2
DEC_BATCH = 8
DEC_SEQ = 4096
PAST_LEN = 256

GRID_W = 64
D_MIX = D_MODEL
POOL_WIDTH = D_MIX // 4
POOL_WINDOWS = (2, 4, 8, 16)
POOL_GROUP = POOL_WIDTH // len(POOL_WINDOWS)
FOURIER_WIDTH = D_MIX // 4
FOURIER_GROUPS = 4
FOURIER_GROUP = FOURIER_WIDTH // FOURIER_GROUPS
ATTN_WIDTH = D_MIX - POOL_WIDTH - FOURIER_WIDTH
HEAD_DIM = 64
N_HEADS = ATTN_WIDTH // HEAD_DIM
N_KV_HEADS = 2
GQA_GROUP = N_HEADS // N_KV_HEADS
KV_WIDTH = N_KV_HEADS * HEAD_DIM
IN_WIDTH = POOL_WIDTH + FOURIER_WIDTH + ATTN_WIDTH + 2 * KV_WIDTH
WINDOW = 128
ATTN_BLOCK = 128
ROPE_THETA = 10000.0
ROPE_HALF = HEAD_DIM // 2
PEER_HEADS = 8
PEER_NKEYS = 128
PEER_EXPERTS = PEER_NKEYS * PEER_NKEYS
PEER_TOPK = 16
PEER_QDIM = 256
PEER_HALF = PEER_QDIM // 2
PEER_CHUNK = 128
NORM_EPS = 1e-6
NEG_INF = -1e30

kernel_name = "hybrid_pool_fourier_swa_peer_diffusion_step"


def _rmsnorm(x, g):
    xf = x.astype(jnp.float32)
    y = xf * lax.rsqrt(jnp.mean(xf * xf, axis=-1, keepdims=True) + NORM_EPS)
    return (y * g.astype(jnp.float32)).astype(x.dtype)


def _rope_axis(xh, pos):
    inv_freq = ROPE_THETA ** (-(jnp.arange(ROPE_HALF // 2, dtype=jnp.float32) * 2.0 / ROPE_HALF))
    ang = pos.astype(jnp.float32)[:, None] * inv_freq[None, :]
    cos = jnp.cos(ang)[:, None, :]
    sin = jnp.sin(ang)[:, None, :]
    x1, x2 = xh[..., : ROPE_HALF // 2], xh[..., ROPE_HALF // 2:]
    return jnp.concatenate([x1 * cos - x2 * sin, x2 * cos + x1 * sin], axis=-1)


def _axial_rope(x):
    S = x.shape[1]
    rows = S // GRID_W
    row = jnp.repeat(jnp.arange(rows, dtype=jnp.int32), GRID_W)
    col = jnp.tile(jnp.arange(GRID_W, dtype=jnp.int32), rows)
    xf = x.astype(jnp.float32)
    out = jnp.concatenate([_rope_axis(xf[..., :ROPE_HALF], row), _rope_axis(xf[..., ROPE_HALF:], col)], axis=-1)
    return out.astype(x.dtype)


def _pool_mixer(xp, w_pool_l, scale_l):
    B, S, _ = xp.shape
    xf = xp.astype(jnp.float32)
    cs = jnp.concatenate([jnp.zeros((B, 1, POOL_WIDTH), jnp.float32), jnp.cumsum(xf, axis=1)], axis=1)
    t = jnp.arange(S, dtype=jnp.int32)
    outs = []
    for gi, w in enumerate(POOL_WINDOWS):
        lo = jnp.clip(t - w // 2, 0, S)
        hi = jnp.clip(t + w // 2, 0, S)
        cs_g = cs[..., gi * POOL_GROUP:(gi + 1) * POOL_GROUP]
        cnt = (hi - lo).astype(jnp.float32)[None, :, None]
        mean = (cs_g[:, hi] - cs_g[:, lo]) / cnt
        outs.append(mean - xf[..., gi * POOL_GROUP:(gi + 1) * POOL_GROUP])
    y = jnp.stack(outs, axis=2).astype(xp.dtype)
    y = jnp.einsum('bsgc,gcd->bsgd', y, w_pool_l).reshape(B, S, POOL_WIDTH)
    return y * scale_l


def _fourier_mixer(xq, w_four_l):
    B, S, _ = xq.shape
    xg = xq.reshape(B, S, FOURIER_GROUPS, FOURIER_GROUP).astype(jnp.float32)
    f = jnp.fft.fft2(xg, axes=(1, 3), norm='ortho').real.astype(xq.dtype)
    return jnp.einsum('bsgc,gcd->bsgd', f, w_four_l).reshape(B, S, FOURIER_WIDTH)


def _sink_column(sink_l, shape):
    s = sink_l.reshape(N_KV_HEADS, GQA_GROUP).astype(jnp.float32)[None, :, :, None, None]
    return jnp.broadcast_to(s, shape[:-1] + (1,))


def _context_attention(q, k, v, sink_l):
    B, C = q.shape[0], q.shape[1]
    qg = q.reshape(B, C, N_KV_HEADS, GQA_GROUP, HEAD_DIM) * (HEAD_DIM ** -0.5)
    s = jnp.einsum('bqhgd,bkhd->bhgqk', qg, k).astype(jnp.float32)
    s = jnp.concatenate([s, _sink_column(sink_l, s.shape)], axis=-1)
    p = jax.nn.softmax(s, axis=-1)[..., :C].astype(v.dtype)
    out = jnp.einsum('bhgqk,bkhd->bqhgd', p, v)
    return out.reshape(B, C, ATTN_WIDTH)


def _latent_attention(q, k, v, ck, cv, sink_l):
    B, S = q.shape[0], q.shape[1]
    C = ck.shape[1]
    nb = S // ATTN_BLOCK
    qg = q.reshape(B, nb, ATTN_BLOCK, N_KV_HEADS, GQA_GROUP, HEAD_DIM) * (HEAD_DIM ** -0.5)
    qb = jnp.transpose(qg, (1, 0, 2, 3, 4, 5))

    def band(t):
        tp = jnp.pad(t, ((0, 0), (ATTN_BLOCK, ATTN_BLOCK), (0, 0), (0, 0)))
        tp = tp.reshape(B, nb + 2, ATTN_BLOCK, N_KV_HEADS, HEAD_DIM)
        tw = jnp.concatenate([tp[:, :-2], tp[:, 1:-1], tp[:, 2:]], axis=2)
        return jnp.transpose(tw, (1, 0, 2, 3, 4))

    kw, vw = band(k), band(v)
    qi = jnp.arange(ATTN_BLOCK, dtype=jnp.int32)
    kj = jnp.arange(3 * ATTN_BLOCK, dtype=jnp.int32)
    rel = kj[None, :] - ATTN_BLOCK - qi[:, None]
    in_band = jnp.abs(rel) <= WINDOW
    pos_k = jnp.arange(nb, dtype=jnp.int32)[:, None] * ATTN_BLOCK - ATTN_BLOCK + kj[None, :]
    valid = (pos_k >= 0) & (pos_k < S)
    mask = in_band[None] & valid[:, None, :]

    def step(args):
        qblk, kblk, vblk, m = args
        s_loc = jnp.einsum('bqhgd,bjhd->bhgqj', qblk, kblk).astype(jnp.float32)
        s_loc = jnp.where(m[None, None, None], s_loc, NEG_INF)
        s_ctx = jnp.einsum('bqhgd,bchd->bhgqc', qblk, ck).astype(jnp.float32)
        s = jnp.concatenate([s_loc, s_ctx, _sink_column(sink_l, s_loc.shape)], axis=-1)
        p = jax.nn.softmax(s, axis=-1)
        p_loc = p[..., :3 * ATTN_BLOCK].astype(vblk.dtype)
        p_ctx = p[..., 3 * ATTN_BLOCK:3 * ATTN_BLOCK + C].astype(cv.dtype)
        return jnp.einsum('bhgqj,bjhd->bqhgd', p_loc, vblk) + jnp.einsum('bhgqc,bchd->bqhgd', p_ctx, cv)

    out = lax.map(step, (qb, kw, vw, mask))
    return jnp.transpose(out, (1, 0, 2, 3, 4, 5)).reshape(B, S, ATTN_WIDTH)


def _peer(h, wq_l, sk_l, u_l, v_l):
    B, S, D = h.shape
    hc = h.reshape((B * S) // PEER_CHUNK, PEER_CHUNK, D)

    def step(xc):
        q = (xc @ wq_l).reshape(PEER_CHUNK, PEER_HEADS, 2, PEER_HALF)
        s = jnp.einsum('thpd,hpkd->thpk', q, sk_l).astype(jnp.float32)
        v1, i1 = lax.top_k(s[:, :, 0], PEER_TOPK)
        v2, i2 = lax.top_k(s[:, :, 1], PEER_TOPK)
        cand = (v1[..., :, None] + v2[..., None, :]).reshape(PEER_CHUNK, PEER_HEADS, PEER_TOPK * PEER_TOPK)
        best, ci = lax.top_k(cand, PEER_TOPK)
        e1 = jnp.take_along_axis(i1, ci // PEER_TOPK, axis=-1)
        e2 = jnp.take_along_axis(i2, ci % PEER_TOPK, axis=-1)
        idx = e1 * PEER_NKEYS + e2
        g = jax.nn.softmax(best, axis=-1).astype(xc.dtype)
        ug = u_l[idx]
        vg = v_l[idx]
        a = jax.nn.gelu(jnp.einsum('td,thkd->thk', xc, ug))
        return jnp.einsum('thk,thkd->td', g * a, vg)

    return lax.map(step, hc).reshape(B, S, D)


def _layer(x, mod, ln_mix, ln_ffn, w_in_l, w_pool_l, pool_scale_l, w_four_l, sink_l, w_out_l,
           wq_l, sk_l, u_l, v_l, ctx_k=None, ctx_v=None):
    B, S, _ = x.shape
    sh1, sc1, g1, sh2, sc2, g2 = jnp.split(mod[:, None, :].astype(x.dtype), 6, axis=-1)
    h = _rmsnorm(x, ln_mix) * (1 + sc1) + sh1
    proj = h @ w_in_l
    o1 = POOL_WIDTH
    o2 = o1 + FOURIER_WIDTH
    o3 = o2 + ATTN_WIDTH
    o4 = o3 + KV_WIDTH
    p_pool = proj[..., :o1]
    p_four = proj[..., o1:o2]
    q = proj[..., o2:o3].reshape(B, S, N_HEADS, HEAD_DIM)
    k = proj[..., o3:o4].reshape(B, S, N_KV_HEADS, HEAD_DIM)
    v = proj[..., o4:].reshape(B, S, N_KV_HEADS, HEAD_DIM)
    pool_out = _pool_mixer(p_pool, w_pool_l, pool_scale_l)
    four_out = _fourier_mixer(p_four, w_four_l)
    if ctx_k is None:
        attn_out = _context_attention(q, k, v, sink_l)
    else:
        attn_out = _latent_attention(_axial_rope(q), _axial_rope(k), v, ctx_k, ctx_v, sink_l)
    mixed = jnp.concatenate([pool_out, four_out, attn_out], axis=-1) @ w_out_l
    x = x + g1 * mixed
    h2 = _rmsnorm(x, ln_ffn) * (1 + sc2) + sh2
    x = x + g2 * _peer(h2, wq_l, sk_l, u_l, v_l)
    return x, k, v


def setup_inputs(seed: int = 0) -> dict:
    key = jax.random.key(seed)
    ks = jax.random.split(key, 24)
    f32 = jnp.float32
    nrm = lambda k, shape, s: jax.random.normal(k, shape, f32) * s
    return {
        'x_prompt': nrm(ks[0], (BATCH, SEQ, D_MODEL), 1.0),
        'x_sample': nrm(ks[1], (DEC_BATCH, DEC_SEQ, D_MODEL), 1.0),
        'cache_k': nrm(ks[2], (DEC_BATCH, DEPTH, PAST_LEN, N_KV_HEADS, HEAD_DIM), 1.0),
        'cache_v': nrm(ks[3], (DEC_BATCH, DEPTH, PAST_LEN, N_KV_HEADS, HEAD_DIM), 1.0),
        'c': nrm(ks[4], (DEC_BATCH, D_MODEL), 1.0),
        'c_ctx': nrm(ks[5], (D_MODEL,), 1.0),
        'w_ada': nrm(ks[6], (DEPTH, D_MODEL, 6 * D_MODEL), 0.5 * D_MODEL ** -0.5),
        'b_ada': nrm(ks[7], (DEPTH, 6 * D_MODEL), 0.02),
        'norm_mix': 1.0 + nrm(ks[8], (DEPTH, D_MODEL), 0.02),
        'norm_ffn': 1.0 + nrm(ks[9], (DEPTH, D_MODEL), 0.02),
        'w_in': nrm(ks[10], (DEPTH, D_MODEL, IN_WIDTH), D_MODEL ** -0.5),
        'w_pool': nrm(ks[11], (DEPTH, len(POOL_WINDOWS), POOL_GROUP, POOL_GROUP), POOL_GROUP ** -0.5),
        'pool_scale': 1.0 + nrm(ks[12], (DEPTH, POOL_WIDTH), 0.02),
        'w_fourier': nrm(ks[13], (DEPTH, FOURIER_GROUPS, FOURIER_GROUP, FOURIER_GROUP), FOURIER_GROUP ** -0.5),
        'attn_sink': nrm(ks[14], (DEPTH, N_HEADS), 1.0),
        'w_out': nrm(ks[15], (DEPTH, D_MIX, D_MODEL), D_MIX ** -0.5),
        'peer_w_query': nrm(ks[16], (DEPTH, D_MODEL, PEER_HEADS * PEER_QDIM), D_MODEL ** -0.5),
        'peer_sub_keys': nrm(ks[17], (DEPTH, PEER_HEADS, 2, PEER_NKEYS, PEER_HALF), PEER_HALF ** -0.5),
        'peer_u': nrm(ks[18], (DEPTH, PEER_EXPERTS, D_MODEL), D_MODEL ** -0.5),
        'peer_v': nrm(ks[19], (DEPTH, PEER_EXPERTS, D_MODEL), PEER_HEADS ** -0.5),
        'final_norm': 1.0 + nrm(ks[20], (D_MODEL,), 0.02),
    }


def reference(x_prompt, x_sample, cache_k, cache_v, c, c_ctx, w_ada, b_ada, norm_mix, norm_ffn,
              w_in, w_pool, pool_scale, w_fourier, attn_sink, w_out, peer_w_query, peer_sub_keys,
              peer_u, peer_v, final_norm):
    xp = x_prompt
    xs = x_sample
    new_k = []
    new_v = []
    for l in range(DEPTH):
        mod_ctx = jax.nn.silu(c_ctx)[None, :] @ w_ada[l] + b_ada[l]
        mod_lat = jax.nn.silu(c) @ w_ada[l] + b_ada[l]
        wl = (norm_mix[l], norm_ffn[l], w_in[l], w_pool[l], pool_scale[l], w_fourier[l], attn_sink[l],
              w_out[l], peer_w_query[l], peer_sub_keys[l], peer_u[l], peer_v[l])
        xp, k_l, v_l = _layer(xp, mod_ctx, *wl)
        new_k.append(k_l)
        new_v.append(v_l)
        xs, _, _ = _layer(xs, mod_lat, *wl, ctx_k=cache_k[:, l], ctx_v=cache_v[:, l])
    y_prompt = _rmsnorm(xp, final_norm)
    y_sample = _rmsnorm(xs, final_norm)
    new_cache_k = jnp.stack(new_k, axis=1)
    new_cache_v = jnp.stack(new_v, axis=1)
    return (y_prompt, y_sample, new_cache_k, new_cache_v)
```

```python
import functools
import math

import numpy as np
import jax
import jax.numpy as jnp
from jax import lax
from jax.experimental import pallas as pl
from jax.experimental.pallas import tpu as pltpu

F32 = jnp.float32
BF16 = jnp.bfloat16

HEAD_DIM = 64
N_KV_HEADS = 2
POOL_WINDOWS = (2, 4, 8, 16)
POOL_HALO = max(POOL_WINDOWS) // 2
FOURIER_GROUPS = 4
GRID_W = 64
WINDOW = 128
ATTN_BLOCK = 128
ROPE_THETA = 10000.0
ROPE_HALF = HEAD_DIM // 2
PEER_TOPK = 16
NORM_EPS = 1e-6
NEG_INF = -1e30
DFT_ROWS = 64
MOD_ROWS = 16
V7X_VMEM_LIMIT = 48 * 1024 * 1024


def _params(semantics):
    return pltpu.CompilerParams(dimension_semantics=semantics, vmem_limit_bytes=V7X_VMEM_LIMIT)


def _dot(a, b):
    return jnp.dot(a, b, preferred_element_type=F32)


def _dot_nt(a, b):
    return lax.dot_general(a, b, (((1,), (1,)), ((), ())), preferred_element_type=F32)


def _rms(x, g):
    return x * lax.rsqrt(jnp.mean(x * x, axis=-1, keepdims=True) + NORM_EPS) * g


def _ada_kernel(c_ref, w_ref, b_ref, o_ref):
    cv = c_ref[...]
    act = cv * (1.0 / (1.0 + jnp.exp(-cv)))
    o_ref[0] = _dot(act.astype(BF16), w_ref[0].astype(BF16)) + b_ref[0]


def _ada(cvec, w_ada, b_ada):
    L, D, N = w_ada.shape
    tn = N // 4
    return pl.pallas_call(
        _ada_kernel,
        grid=(L, N // tn),
        in_specs=[pl.BlockSpec((MOD_ROWS, D), lambda l, j: (0, 0)),
                  pl.BlockSpec((1, D, tn), lambda l, j: (l, 0, j)),
                  pl.BlockSpec((1, 1, tn), lambda l, j: (l, 0, j))],
        out_specs=pl.BlockSpec((1, MOD_ROWS, tn), lambda l, j: (l, 0, j)),
        out_shape=jax.ShapeDtypeStruct((L, MOD_ROWS, N), F32),
        compiler_params=_params(("parallel", "parallel")),
        name="ada_mod",
    )(cvec, w_ada, b_ada.reshape(L, 1, N))


def _rope_apply(x, cos, sin):
    lane = lax.broadcasted_iota(jnp.int32, (x.shape[0], 128), 1)
    first = (lane % ROPE_HALF) < (ROPE_HALF // 2)
    outs = []
    for g in range(x.shape[1] // 128):
        xg = x[:, g * 128:(g + 1) * 128]
        partner = jnp.where(first, pltpu.roll(xg, 128 - ROPE_HALF // 2, axis=1),
                            pltpu.roll(xg, ROPE_HALF // 2, axis=1))
        outs.append(xg * cos + partner * sin)
    return outs[0] if len(outs) == 1 else jnp.concatenate(outs, axis=1)


def _in_proj_kernel(*refs, D, PW, FW, AW, KW, rope):
    if rope:
        x_ref, mod_ref, ln_ref, w_ref, cs_ref, cos_ref, sin_ref, pool_ref, xcs_ref, q_ref, k_ref, v_ref = refs
    else:
        x_ref, mod_ref, ln_ref, w_ref, cs_ref, pool_ref, xcs_ref, q_ref, k_ref, v_ref = refs
    mod = mod_ref[0]
    sh1 = mod[:, 0:D]
    sc1 = mod[:, D:2 * D]
    h = _rms(x_ref[...], ln_ref[...]) * (1.0 + sc1) + sh1
    proj = _dot(h.astype(BF16), w_ref[...])
    o1, o2 = PW, PW + FW
    o3 = o2 + AW
    o4 = o3 + KW
    pool_ref[...] = proj[:, :o1]
    xcs_ref[...] = _dot(proj[:, o1:o2].astype(BF16), cs_ref[...]).astype(BF16)
    q = proj[:, o2:o3]
    k = proj[:, o3:o4]
    if rope:
        q = _rope_apply(q, cos_ref[...], sin_ref[...])
        k = _rope_apply(k, cos_ref[...], sin_ref[...])
    q_ref[...] = (q * (HEAD_DIM ** -0.5)).astype(BF16)
    k_ref[...] = k
    v_ref[...] = proj[:, o4:]


def _mod_spec(n_cols, tile, seq_len, per_seq):
    tiles_per_seq = max(seq_len // tile, 1)
    if per_seq:
        return pl.BlockSpec((1, 1, n_cols), lambda i, *_: (1 + i // tiles_per_seq, 0, 0))
    return pl.BlockSpec((1, 1, n_cols), lambda i, *_: (0, 0, 0))


def _in_proj(x, mod3, ln, w_in, cs, rope_tabs, seq_len, per_seq, dims):
    T, D = x.shape
    PW, FW, AW, KW = dims
    tm = min(256, seq_len)
    rope = rope_tabs is not None
    n_in = w_in.shape[1]
    tiles_per_seq = seq_len // tm
    in_specs = [pl.BlockSpec((tm, D), lambda i: (i, 0)),
                _mod_spec(6 * D, tm, seq_len, per_seq),
                pl.BlockSpec((1, D), lambda i: (0, 0)),
                pl.BlockSpec((D, n_in), lambda i: (0, 0)),
                pl.BlockSpec((FW, 2 * FW), lambda i: (0, 0))]
    args = [x, mod3, ln, w_in, cs]
    if rope:
        in_specs += [pl.BlockSpec((tm, 128), lambda i: (i % tiles_per_seq, 0))] * 2
        args += list(rope_tabs)
    outs = [(PW, F32), (2 * FW, BF16), (AW, BF16), (KW, F32), (KW, F32)]
    return pl.pallas_call(
        functools.partial(_in_proj_kernel, D=D, PW=PW, FW=FW, AW=AW, KW=KW, rope=rope),
        grid=(T // tm,),
        in_specs=in_specs,
        out_specs=[pl.BlockSpec((tm, w), lambda i: (i, 0)) for w, _ in outs],
        out_shape=[jax.ShapeDtypeStruct((T, w), dt) for w, dt in outs],
        compiler_params=_params(("parallel",)),
        name="in_proj_rope" if rope else "in_proj",
    )(*args)


def _pool_kernel(xp_ref, w_ref, sc_ref, o_ref, *, S, TP):
    c = pl.program_id(1)
    n = TP + 2 * POOL_HALO
    xs = xp_ref[0, pl.ds(pl.multiple_of(c * TP, 8), n), :]
    width = xs.shape[1]
    run = xs
    centred = []
    for w in POOL_WINDOWS:
        run = run + pltpu.roll(run, n - w // 2, axis=0)
        centred.append(pltpu.roll(run, w // 2, axis=0)[POOL_HALO:POOL_HALO + TP])
    lane = lax.broadcasted_iota(jnp.int32, (TP, width), 1)
    grp = lane // (width // len(POOL_WINDOWS))
    win = centred[-1]
    half = jnp.full((TP, width), POOL_WINDOWS[-1] // 2, jnp.int32)
    for gi in range(len(POOL_WINDOWS) - 2, -1, -1):
        win = jnp.where(grp == gi, centred[gi], win)
        half = jnp.where(grp == gi, POOL_WINDOWS[gi] // 2, half)
    t = c * TP + lax.broadcasted_iota(jnp.int32, (TP, width), 0)
    cnt = (jnp.clip(t + half, 0, S) - jnp.clip(t - half, 0, S)).astype(F32)
    y = win / cnt - xs[POOL_HALO:POOL_HALO + TP]
    o_ref[0] = (_dot(y.astype(BF16), w_ref[...]) * sc_ref[...]).astype(BF16)


def _pool(p_pool, w_bd, scale):
    B, S, W = p_pool.shape
    tp = min(256, S)
    xp = jnp.pad(p_pool, ((0, 0), (POOL_HALO, POOL_HALO), (0, 0)))
    return pl.pallas_call(
        functools.partial(_pool_kernel, S=S, TP=tp),
        grid=(B, S // tp),
        in_specs=[pl.BlockSpec((1, S + 2 * POOL_HALO, W), lambda b, c: (b, 0, 0)),
                  pl.BlockSpec((W, W), lambda b, c: (0, 0)),
                  pl.BlockSpec((1, W), lambda b, c: (0, 0))],
        out_specs=pl.BlockSpec((1, tp, W), lambda b, c: (b, c, 0)),
        out_shape=jax.ShapeDtypeStruct((B, S, W), BF16),
        compiler_params=_params(("parallel", "arbitrary")),
        name="pool_mixer",
    )(xp, w_bd, scale)


def _dft_kernel(ca_ref, sa_ref, cb_ref, sb_ref, c_ref, s_ref):
    ca, sa = ca_ref[0], sa_ref[0]
    cb, sb = cb_ref[...], sb_ref[...]
    c_ref[...] = (ca * cb - sa * sb).astype(BF16)
    s_ref[...] = (sa * cb + ca * sb).astype(BF16)


def _dft_tables(S):
    R = min(DFT_ROWS, S)
    k = np.arange(S, dtype=np.int64)
    a = np.arange(S // R, dtype=np.int64)[:, None]
    b = np.arange(R, dtype=np.int64)[:, None]
    ang_a = 2.0 * np.pi * ((a * R * k) % S) / S
    ang_b = 2.0 * np.pi * ((b * k) % S) / S
    ca = jnp.asarray(np.cos(ang_a), F32).reshape(S // R, 1, S)
    sa = jnp.asarray(np.sin(ang_a), F32).reshape(S // R, 1, S)
    cb = jnp.asarray(np.cos(ang_b), F32)
    sb = jnp.asarray(np.sin(ang_b), F32)
    row = pl.BlockSpec((1, 1, S), lambda i: (i, 0, 0))
    full = pl.BlockSpec((R, S), lambda i: (0, 0))
    out = pl.BlockSpec((R, S), lambda i: (i, 0))
    return pl.pallas_call(
        _dft_kernel,
        grid=(S // R,),
        in_specs=[row, row, full, full],
        out_specs=[out, out],
        out_shape=[jax.ShapeDtypeStruct((S, S), BF16)] * 2,
        compiler_params=_params(("parallel",)),
        name="dft_tables",
    )(ca, sa, cb, sb)


def _channel_dft(FW):
    gw = FW // FOURIER_GROUPS
    k = np.arange(gw)
    ang = 2.0 * np.pi * ((k[:, None] * k[None, :]) % gw) / gw
    cs = np.zeros((FW, 2 * FW), np.float64)
    for g in range(FOURIER_GROUPS):
        cs[g * gw:(g + 1) * gw, g * gw:(g + 1) * gw] = np.cos(ang)
        cs[g * gw:(g + 1) * gw, FW + g * gw:FW + (g + 1) * gw] = np.sin(ang)
    return jnp.asarray(cs, BF16)


def _four_kernel(c_ref, s_ref, xcs_ref, wf_ref, o_ref, *, FW, scale):
    y = _dot(c_ref[...], xcs_ref[0, :, :FW]) - _dot(s_ref[...], xcs_ref[0, :, FW:])
    o_ref[0] = _dot((y * scale).astype(BF16), wf_ref[...]).astype(BF16)


def _fourier(xcs, tabs, wf_bd):
    B, S, FW2 = xcs.shape
    FW = FW2 // 2
    tq = min(512, S)
    scale = 1.0 / math.sqrt(S * (FW // FOURIER_GROUPS))
    return pl.pallas_call(
        functools.partial(_four_kernel, FW=FW, scale=scale),
        grid=(S // tq, B),
        in_specs=[pl.BlockSpec((tq, S), lambda i, b: (i, 0)),
                  pl.BlockSpec((tq, S), lambda i, b: (i, 0)),
                  pl.BlockSpec((1, S, FW2), lambda i, b: (b, 0, 0)),
                  pl.BlockSpec((FW, FW), lambda i, b: (0, 0))],
        out_specs=pl.BlockSpec((1, tq, FW), lambda i, b: (b, i, 0)),
        out_shape=jax.ShapeDtypeStruct((B, S, FW), BF16),
        compiler_params=_params(("parallel", "arbitrary")),
        name="fourier_mixer",
    )(tabs[0], tabs[1], xcs, wf_bd)


def _softmax_pv(s, sink, v):
    m = jnp.maximum(jnp.max(s, axis=-1, keepdims=True), sink)
    p = jnp.exp(s - m)
    den = jnp.sum(p, axis=-1, keepdims=True) + jnp.exp(sink - m)
    return _dot(p.astype(BF16), v) / den


def _attn_ctx_kernel(sink_ref, q_ref, k_ref, v_ref, o_ref, *, G):
    q = q_ref[0]
    k = k_ref[0].astype(BF16)
    v = v_ref[0].astype(BF16)
    for kh in range(N_KV_HEADS):
        kk = k[:, kh * HEAD_DIM:(kh + 1) * HEAD_DIM]
        vv = v[:, kh * HEAD_DIM:(kh + 1) * HEAD_DIM]
        for g in range(G):
            hd = kh * G + g
            s = _dot_nt(q[:, hd * HEAD_DIM:(hd + 1) * HEAD_DIM], kk)
            o = _softmax_pv(s, sink_ref[hd], vv)
            o_ref[0, :, hd * HEAD_DIM:(hd + 1) * HEAD_DIM] = o.astype(BF16)


def _attn_ctx(sink, q, k, v):
    B, S, AW = q.shape
    KW = k.shape[2]
    G = AW // HEAD_DIM // N_KV_HEADS
    return pl.pallas_call(
        functools.partial(_attn_ctx_kernel, G=G),
        grid=(B,),
        in_specs=[pl.BlockSpec(memory_space=pltpu.SMEM),
                  pl.BlockSpec((1, S, AW), lambda b: (b, 0, 0)),
                  pl.BlockSpec((1, S, KW), lambda b: (b, 0, 0)),
                  pl.BlockSpec((1, S, KW), lambda b: (b, 0, 0))],
        out_specs=pl.BlockSpec((1, S, AW), lambda b: (b, 0, 0)),
        out_shape=jax.ShapeDtypeStruct((B, S, AW), BF16),
        compiler_params=_params(("parallel",)),
        name="attn_context",
    )(sink, q, k, v)


def _attn_lat_kernel(sink_ref, q_ref, kp_ref, kc_ref, kn_ref, vp_ref, vc_ref, vn_ref, ck_ref, cv_ref, o_ref,
                     *, S, G):
    i = pl.program_id(1)
    blk = ATTN_BLOCK
    q = q_ref[0]
    kcat = jnp.concatenate([kp_ref[0], kc_ref[0], kn_ref[0], ck_ref[0]], axis=0).astype(BF16)
    vcat = jnp.concatenate([vp_ref[0], vc_ref[0], vn_ref[0], cv_ref[0]], axis=0).astype(BF16)
    nk = kcat.shape[0]
    qi = lax.broadcasted_iota(jnp.int32, (blk, nk), 0)
    kj = lax.broadcasted_iota(jnp.int32, (blk, nk), 1)
    pos = i * blk - blk + kj
    local_ok = jnp.where(jnp.abs(kj - blk - qi) <= WINDOW,
                         jnp.where(pos >= 0, jnp.where(pos < S, 1, 0), 0), 0)
    ok = jnp.where(kj >= 3 * blk, 1, local_ok) > 0
    ok = jnp.concatenate([ok] * G, axis=0)
    for kh in range(N_KV_HEADS):
        kk = kcat[:, kh * HEAD_DIM:(kh + 1) * HEAD_DIM]
        vv = vcat[:, kh * HEAD_DIM:(kh + 1) * HEAD_DIM]
        qs = jnp.concatenate([q[:, (kh * G + g) * HEAD_DIM:(kh * G + g + 1) * HEAD_DIM] for g in range(G)], axis=0)
        sink = jnp.concatenate([jnp.full((blk, 1), sink_ref[kh * G + g], F32) for g in range(G)], axis=0)
        s = jnp.where(ok, _dot_nt(qs, kk), NEG_INF)
        o = _softmax_pv(s, sink, vv)
        for g in range(G):
            hd = kh * G + g
            o_ref[0, :, hd * HEAD_DIM:(hd + 1) * HEAD_DIM] = o[g * blk:(g + 1) * blk].astype(BF16)


def _attn_lat(sink, q, k, v, ck, cv):
    B, S, AW = q.shape
    KW = k.shape[2]
    C = ck.shape[1]
    G = AW // HEAD_DIM // N_KV_HEADS
    nb = S // ATTN_BLOCK
    prev = pl.BlockSpec((1, ATTN_BLOCK, KW), lambda b, i: (b, jnp.maximum(i - 1, 0), 0))
    cur = pl.BlockSpec((1, ATTN_BLOCK, KW), lambda b, i: (b, i, 0))
    nxt = pl.BlockSpec((1, ATTN_BLOCK, KW), lambda b, i: (b, jnp.minimum(i + 1, nb - 1), 0))
    ctx = pl.BlockSpec((1, C, KW), lambda b, i: (b, 0, 0))
    return pl.pallas_call(
        functools.partial(_attn_lat_kernel, S=S, G=G),
        grid=(B, nb),
        in_specs=[pl.BlockSpec(memory_space=pltpu.SMEM),
                  pl.BlockSpec((1, ATTN_BLOCK, AW), lambda b, i: (b, i, 0)),
                  prev, cur, nxt, prev, cur, nxt, ctx, ctx],
        out_specs=pl.BlockSpec((1, ATTN_BLOCK, AW), lambda b, i: (b, i, 0)),
        out_shape=jax.ShapeDtypeStruct((B, S, AW), BF16),
        compiler_params=_params(("parallel", "parallel")),
        name="attn_latent",
    )(sink, q, k, k, k, v, v, v, ck, cv)


def _out_proj_kernel(pool_ref, four_ref, attn_ref, x_ref, mod_ref, ln_ref, w_ref, x1_ref, h2t_ref, *, D, PW, FW):
    mixed = (_dot(pool_ref[...], w_ref[0:PW, :]) + _dot(four_ref[...], w_ref[PW:PW + FW, :])
             + _dot(attn_ref[...], w_ref[PW + FW:, :]))
    mod = mod_ref[0]
    g1 = mod[:, 2 * D:3 * D]
    sh2 = mod[:, 3 * D:4 * D]
    sc2 = mod[:, 4 * D:5 * D]
    x1 = x_ref[...] + g1 * mixed
    h2 = _rms(x1, ln_ref[...]) * (1.0 + sc2) + sh2
    x1_ref[...] = x1
    h2t_ref[...] = h2.T.astype(BF16)


def _out_proj(pool_o, four_o, attn_o, x, mod3, ln, w_out, seq_len, per_seq):
    T, D = x.shape
    PW, FW, AW = pool_o.shape[1], four_o.shape[1], attn_o.shape[1]
    tm = min(256, seq_len)
    return pl.pallas_call(
        functools.partial(_out_proj_kernel, D=D, PW=PW, FW=FW),
        grid=(T // tm,),
        in_specs=[pl.BlockSpec((tm, PW), lambda i: (i, 0)),
                  pl.BlockSpec((tm, FW), lambda i: (i, 0)),
                  pl.BlockSpec((tm, AW), lambda i: (i, 0)),
                  pl.BlockSpec((tm, D), lambda i: (i, 0)),
                  _mod_spec(6 * D, tm, seq_len, per_seq),
                  pl.BlockSpec((1, D), lambda i: (0, 0)),
                  pl.BlockSpec((PW + FW + AW, D), lambda i: (0, 0))],
        out_specs=[pl.BlockSpec((tm, D), lambda i: (i, 0)),
                   pl.BlockSpec((D, tm), lambda i: (0, i))],
        out_shape=[jax.ShapeDtypeStruct((T, D), F32), jax.ShapeDtypeStruct((D, T), BF16)],
        compiler_params=_params(("parallel",)),
        name="out_proj",
    )(pool_o, four_o, attn_o, x, mod3, ln, w_out)


def _top_values(vals, k):
    out = []
    for r in range(k):
        m = jnp.max(vals, axis=0, keepdims=True)
        out.append(m)
        if r + 1 < k:
            vals = jnp.where(vals == m, -jnp.inf, vals)
    return out


def _peer_scores_kernel(xt_ref, wqt_ref, sk_ref, a1_ref, cut_ref, a2_ref, s2_ref, qt_ref, cand_ref, *, H, K):
    half = sk_ref.shape[3]
    qt_ref[...] = _dot(wqt_ref[...], xt_ref[...]).astype(BF16)
    pairs = [(i, j) for i in range(K) for j in range(K) if (i + 1) * (j + 1) <= K]
    cand_ref[...] = jnp.full(cand_ref.shape, -jnp.inf, F32)
    for h in range(H):
        s1 = _dot(sk_ref[h, 0], qt_ref[(2 * h) * half:(2 * h + 1) * half, :])
        s2 = _dot(sk_ref[h, 1], qt_ref[(2 * h + 1) * half:(2 * h + 2) * half, :])
        v1 = _top_values(s1, K)
        v2 = _top_values(s2, K)
        sums = {}
        for n, (i, j) in enumerate(pairs):
            sums[(i, j)] = v1[i] + v2[j]
            cand_ref[n:n + 1, :] = sums[(i, j)]
        best = _top_values(cand_ref[...], K)
        theta, top = best[K - 1], best[0]
        z = jnp.exp(best[0] - top)
        for r in range(1, K):
            z = z + jnp.exp(best[r] - top)
        cut = jnp.full(s1.shape, jnp.inf, F32)
        for i in range(K):
            ci = jnp.full(theta.shape, jnp.inf, F32)
            for j in range(K):
                if (i, j) in sums:
                    ci = jnp.minimum(ci, jnp.where(sums[(i, j)] >= theta, v2[j], jnp.inf))
            cut = jnp.where(s1 == v1[i], ci, cut)
        a1_ref[h] = jnp.exp(s1 - v1[0]) * (1.0 / z)
        cut_ref[h] = cut
        a2_ref[h] = jnp.exp(s2 - v2[0])
        s2_ref[h] = s2


def _peer_scores(h2t, wqt, sk):
    D, T = h2t.shape
    H, _, NK, half = sk.shape
    tn = min(256, T)
    n_pairs = sum(1 for i in range(PEER_TOPK) for j in range(PEER_TOPK) if (i + 1) * (j + 1) <= PEER_TOPK)
    out = jax.ShapeDtypeStruct((H, NK, T), F32)
    ospec = pl.BlockSpec((H, NK, tn), lambda i: (0, 0, i))
    return pl.pallas_call(
        functools.partial(_peer_scores_kernel, H=H, K=PEER_TOPK),
        grid=(T // tn,),
        in_specs=[pl.BlockSpec((D, tn), lambda i: (0, i)),
                  pl.BlockSpec(wqt.shape, lambda i: (0, 0)),
                  pl.BlockSpec(sk.shape, lambda i: (0, 0, 0, 0))],
        out_specs=[ospec] * 4,
        out_shape=[out] * 4,
        scratch_shapes=[pltpu.VMEM((wqt.shape[0], tn), BF16),
                        pltpu.VMEM((-(-n_pairs // 8) * 8, tn), F32)],
        compiler_params=_params(("parallel",)),
        name="peer_scores",
    )(h2t, wqt, sk)


def _gelu_tanh(x):
    return x * (0.5 * (1.0 + jnp.tanh(math.sqrt(2.0 / math.pi) * (x + 0.044715 * (x * x * x)))))


def _peer_dense_kernel(*refs, D, H, NK, R, final):
    if final:
        xt_ref, u_ref, vt_ref, a1_ref, cut_ref, a2_ref, s2_ref, x1_ref, mod_ref, fn_ref, o_ref, acc_ref, p_ref = refs
    else:
        xt_ref, u_ref, vt_ref, a1_ref, cut_ref, a2_ref, s2_ref, x1_ref, mod_ref, o_ref, acc_ref, p_ref = refs
    j = pl.program_id(1)

    @pl.when(j == 0)
    def _():
        acc_ref[...] = jnp.zeros(acc_ref.shape, F32)

    act = _gelu_tanh(_dot(u_ref[...], xt_ref[...]))
    for r in range(R):
        gate = jnp.zeros((NK, act.shape[1]), F32)
        for h in range(H):
            prod = a1_ref[h, r:r + 1, :] * a2_ref[h]
            gate = gate + jnp.where(s2_ref[h] >= cut_ref[h, r:r + 1, :], prod, 0.0)
        p_ref[r * NK:(r + 1) * NK, :] = (gate * act[r * NK:(r + 1) * NK]).astype(BF16)
    acc_ref[...] += _dot(vt_ref[...], p_ref[...])

    @pl.when(j == pl.num_programs(1) - 1)
    def _():
        g2 = mod_ref[0][:, 5 * D:6 * D]
        x2 = x1_ref[...] + g2 * acc_ref[...].T
        o_ref[...] = _rms(x2, fn_ref[...]) if final else x2


def _peer_dense(h2t, u, vt, a1, cut, a2, s2, x1, mod3, final_norm, seq_len, per_seq):
    D, T = h2t.shape
    E = u.shape[0]
    H, NK, _ = a1.shape
    R = 8
    ec = R * NK
    tn = min(512, seq_len if per_seq else T)
    final = final_norm is not None
    tab = pl.BlockSpec((H, NK, tn), lambda i, j: (0, 0, i))
    row = pl.BlockSpec((H, R, tn), lambda i, j: (0, j, i))
    in_specs = [pl.BlockSpec((D, tn), lambda i, j: (0, i)),
                pl.BlockSpec((ec, D), lambda i, j: (j, 0)),
                pl.BlockSpec((D, ec), lambda i, j: (0, j)),
                row, row, tab, tab,
                pl.BlockSpec((tn, D), lambda i, j: (i, 0)),
                _mod_spec(6 * D, tn, seq_len, per_seq)]
    args = [h2t, u, vt, a1, cut, a2, s2, x1, mod3]
    if final:
        in_specs.append(pl.BlockSpec((1, D), lambda i, j: (0, 0)))
        args.append(final_norm)
    return pl.pallas_call(
        functools.partial(_peer_dense_kernel, D=D, H=H, NK=NK, R=R, final=final),
        grid=(T // tn, E // ec),
        in_specs=in_specs,
        out_specs=pl.BlockSpec((tn, D), lambda i, j: (i, 0)),
        out_shape=jax.ShapeDtypeStruct((T, D), F32),
        scratch_shapes=[pltpu.VMEM((D, tn), F32), pltpu.VMEM((ec, tn), BF16)],
        compiler_params=_params(("parallel", "arbitrary")),
        name="peer_dense_final" if final else "peer_dense",
    )(*args)


def _rope_tables(S):
    pos = np.arange(S)
    inv_freq = ROPE_THETA ** (-(np.arange(ROPE_HALF // 2, dtype=np.float64) * 2.0 / ROPE_HALF))
    d = np.arange(HEAD_DIM)
    p = np.where(d[None, :] < ROPE_HALF, (pos // GRID_W)[:, None], (pos % GRID_W)[:, None])
    ang = p * inv_freq[d % (ROPE_HALF // 2)][None, :]
    sign = np.where((d % ROPE_HALF) < ROPE_HALF // 2, -1.0, 1.0)[None, :]
    cos = np.tile(np.cos(ang), (1, 128 // HEAD_DIM))
    sin = np.tile(np.sin(ang) * sign, (1, 128 // HEAD_DIM))
    return jnp.asarray(cos, F32), jnp.asarray(sin, F32)


def _block_diag(w):
    g, a, b = w.shape
    out = jnp.zeros((g * a, g * b), w.dtype)
    for i in range(g):
        out = out.at[i * a:(i + 1) * a, i * b:(i + 1) * b].set(w[i])
    return out


def kernel(x_prompt, x_sample, cache_k, cache_v, c, c_ctx, w_ada, b_ada, norm_mix, norm_ffn, w_in, w_pool,
           pool_scale, w_fourier, attn_sink, w_out, peer_w_query, peer_sub_keys, peer_u, peer_v, final_norm):
    B, S, D = x_prompt.shape
    BD, SD, _ = x_sample.shape
    L = w_ada.shape[0]
    C = cache_k.shape[2]
    PW = w_pool.shape[1] * w_pool.shape[2]
    FW = w_fourier.shape[1] * w_fourier.shape[2]
    KW = cache_k.shape[3] * cache_k.shape[4]
    AW = w_in.shape[2] - PW - FW - 2 * KW
    dims = (PW, FW, AW, KW)
    assert BD + 1 <= MOD_ROWS and SD % ATTN_BLOCK == 0 and SD % GRID_W == 0

    cvec = jnp.concatenate([c_ctx[None, :], c, jnp.zeros((MOD_ROWS - 1 - BD, D), F32)], axis=0)
    mod = _ada(cvec, w_ada, b_ada).reshape(L, MOD_ROWS, 1, 6 * D)
    cs = _channel_dft(FW)
    tabs_p = _dft_tables(S)
    tabs_s = _dft_tables(SD)
    rope = _rope_tables(SD)

    xp = x_prompt.reshape(B * S, D)
    xs = x_sample.reshape(BD * SD, D)
    new_k, new_v = [], []
    for l in range(L):
        last = l == L - 1
        w_in_l = w_in[l].astype(BF16)
        w_pool_l = _block_diag(w_pool[l]).astype(BF16)
        w_four_l = _block_diag(w_fourier[l]).astype(BF16)
        w_out_l = w_out[l].astype(BF16)
        wqt_l = peer_w_query[l].T.astype(BF16)
        sk_l = peer_sub_keys[l].astype(BF16)
        u_l = peer_u[l].astype(BF16)
        vt_l = peer_v[l].T.astype(BF16)
        ln1 = norm_mix[l][None, :]
        ln2 = norm_ffn[l][None, :]
        scale_l = pool_scale[l][None, :]
        sink_l = attn_sink[l]
        fn = final_norm[None, :] if last else None
        ck = cache_k[:, l].reshape(BD, C, KW)
        cv = cache_v[:, l].reshape(BD, C, KW)

        def layer(x, nb, seq, per_seq, rope_tabs, tabs):
            pool_in, xcs, q, k, v = _in_proj(x, mod[l], ln1, w_in_l, cs, rope_tabs, seq, per_seq, dims)
            pool_o = _pool(pool_in.reshape(nb, seq, PW), w_pool_l, scale_l)
            four_o = _fourier(xcs.reshape(nb, seq, 2 * FW), tabs, w_four_l)
            q3, k3, v3 = q.reshape(nb, seq, AW), k.reshape(nb, seq, KW), v.reshape(nb, seq, KW)
            if per_seq:
                attn_o = _attn_lat(sink_l, q3, k3, v3, ck, cv)
            else:
                attn_o = _attn_ctx(sink_l, q3, k3, v3)
            x1, h2t = _out_proj(pool_o.reshape(-1, PW), four_o.reshape(-1, FW), attn_o.reshape(-1, AW),
                                x, mod[l], ln2, w_out_l, seq, per_seq)
            a1, cut, a2, s2 = _peer_scores(h2t, wqt_l, sk_l)
            x2 = _peer_dense(h2t, u_l, vt_l, a1, cut, a2, s2, x1, mod[l], fn, seq, per_seq)
            return x2, k3, v3

        xp, k_l, v_l = layer(xp, B, S, False, None, tabs_p)
        xs, _, _ = layer(xs, BD, SD, True, rope, tabs_s)
        new_k.append(k_l.reshape(B, S, N_KV_HEADS, HEAD_DIM))
        new_v.append(v_l.reshape(B, S, N_KV_HEADS, HEAD_DIM))
    return (xp.reshape(B, S, D), xs.reshape(BD, SD, D), jnp.stack(new_k, axis=1), jnp.stack(new_v, axis=1))
```

```python
import functools
import math

import numpy as np
import jax
import jax.numpy as jnp
from jax import lax
from jax.experimental import pallas as pl
from jax.experimental.pallas import tpu as pltpu

F32 = jnp.float32
BF16 = jnp.bfloat16

HEAD_DIM = 64
N_KV_HEADS = 2
POOL_WINDOWS = (2, 4, 8, 16)
POOL_HALO = max(POOL_WINDOWS) // 2
FOURIER_GROUPS = 4
GRID_W = 64
WINDOW = 128
ATTN_BLOCK = 128
ROPE_THETA = 10000.0
ROPE_HALF = HEAD_DIM // 2
PEER_TOPK = 16
NORM_EPS = 1e-6
NEG_INF = -1e30
DFT_ROWS = 64
MOD_ROWS = 16
V7X_VMEM_LIMIT = 48 * 1024 * 1024
MXU_DEPTH = 256
PEER_PIECES = 8
GELU_ROWS = 32


def _params(semantics):
    return pltpu.CompilerParams(dimension_semantics=semantics, vmem_limit_bytes=V7X_VMEM_LIMIT)


def _dot(a, b):
    return jnp.dot(a, b, preferred_element_type=F32)


def _dot_nt(a, b):
    return lax.dot_general(a, b, (((1,), (1,)), ((), ())), preferred_element_type=F32)


def _rms(x, g):
    return x * lax.rsqrt(jnp.mean(x * x, axis=-1, keepdims=True) + NORM_EPS) * g


def _ada_kernel(c_ref, w_ref, b_ref, o_ref):
    cv = c_ref[...]
    act = cv * (1.0 / (1.0 + jnp.exp(-cv)))
    o_ref[0] = _dot(act.astype(BF16), w_ref[0].astype(BF16)) + b_ref[0]


def _ada(cvec, w_ada, b_ada):
    L, D, N = w_ada.shape
    tn = N // 4
    return pl.pallas_call(
        _ada_kernel,
        grid=(L, N // tn),
        in_specs=[pl.BlockSpec((MOD_ROWS, D), lambda l, j: (0, 0)),
                  pl.BlockSpec((1, D, tn), lambda l, j: (l, 0, j)),
                  pl.BlockSpec((1, 1, tn), lambda l, j: (l, 0, j))],
        out_specs=pl.BlockSpec((1, MOD_ROWS, tn), lambda l, j: (l, 0, j)),
        out_shape=jax.ShapeDtypeStruct((L, MOD_ROWS, N), F32),
        compiler_params=_params(("parallel", "parallel")),
        name="ada_mod",
    )(cvec, w_ada, b_ada.reshape(L, 1, N))


def _rope_apply(x, cos, sin):
    lane = lax.broadcasted_iota(jnp.int32, (x.shape[0], 128), 1)
    first = (lane % ROPE_HALF) < (ROPE_HALF // 2)
    outs = []
    for g in range(x.shape[1] // 128):
        xg = x[:, g * 128:(g + 1) * 128]
        partner = jnp.where(first, pltpu.roll(xg, 128 - ROPE_HALF // 2, axis=1),
                            pltpu.roll(xg, ROPE_HALF // 2, axis=1))
        outs.append(xg * cos + partner * sin)
    return outs[0] if len(outs) == 1 else jnp.concatenate(outs, axis=1)


def _in_proj_kernel(*refs, D, PW, FW, AW, KW, rope):
    if rope:
        x_ref, mod_ref, ln_ref, w_ref, cs_ref, cos_ref, sin_ref, pool_ref, xcs_ref, q_ref, k_ref, v_ref = refs
    else:
        x_ref, mod_ref, ln_ref, w_ref, cs_ref, pool_ref, xcs_ref, q_ref, k_ref, v_ref = refs
    mod = mod_ref[0]
    sh1 = mod[:, 0:D]
    sc1 = mod[:, D:2 * D]
    h = _rms(x_ref[...], ln_ref[...]) * (1.0 + sc1) + sh1
    proj = _dot(h.astype(BF16), w_ref[...])
    o1, o2 = PW, PW + FW
    o3 = o2 + AW
    o4 = o3 + KW
    pool_ref[...] = proj[:, :o1]
    xcs_ref[...] = _dot(proj[:, o1:o2].astype(BF16), cs_ref[...]).astype(BF16)
    q = proj[:, o2:o3]
    k = proj[:, o3:o4]
    if rope:
        q = _rope_apply(q, cos_ref[...], sin_ref[...])
        k = _rope_apply(k, cos_ref[...], sin_ref[...])
    q_ref[...] = (q * (HEAD_DIM ** -0.5)).astype(BF16)
    k_ref[...] = k
    v_ref[...] = proj[:, o4:]


def _mod_spec(n_cols, tile, seq_len, per_seq):
    tiles_per_seq = max(seq_len // tile, 1)
    if per_seq:
        return pl.BlockSpec((1, 1, n_cols), lambda i, *_: (1 + i // tiles_per_seq, 0, 0))
    return pl.BlockSpec((1, 1, n_cols), lambda i, *_: (0, 0, 0))


def _in_proj(x, mod3, ln, w_in, cs, rope_tabs, seq_len, per_seq, dims):
    T, D = x.shape
    PW, FW, AW, KW = dims
    tm = min(256, seq_len)
    rope = rope_tabs is not None
    n_in = w_in.shape[1]
    tiles_per_seq = seq_len // tm
    in_specs = [pl.BlockSpec((tm, D), lambda i: (i, 0)),
                _mod_spec(6 * D, tm, seq_len, per_seq),
                pl.BlockSpec((1, D), lambda i: (0, 0)),
                pl.BlockSpec((D, n_in), lambda i: (0, 0)),
                pl.BlockSpec((FW, 2 * FW), lambda i: (0, 0))]
    args = [x, mod3, ln, w_in, cs]
    if rope:
        in_specs += [pl.BlockSpec((tm, 128), lambda i: (i % tiles_per_seq, 0))] * 2
        args += list(rope_tabs)
    outs = [(PW, F32), (2 * FW, BF16), (AW, BF16), (KW, F32), (KW, F32)]
    return pl.pallas_call(
        functools.partial(_in_proj_kernel, D=D, PW=PW, FW=FW, AW=AW, KW=KW, rope=rope),
        grid=(T // tm,),
        in_specs=in_specs,
        out_specs=[pl.BlockSpec((tm, w), lambda i: (i, 0)) for w, _ in outs],
        out_shape=[jax.ShapeDtypeStruct((T, w), dt) for w, dt in outs],
        compiler_params=_params(("parallel",)),
        name="in_proj_rope" if rope else "in_proj",
    )(*args)


def _pool_kernel(xp_ref, w_ref, sc_ref, o_ref, *, S, TP):
    c = pl.program_id(1)
    n = TP + 2 * POOL_HALO
    xs = xp_ref[0, pl.ds(pl.multiple_of(c * TP, 8), n), :]
    width = xs.shape[1]
    run = xs
    centred = []
    for w in POOL_WINDOWS:
        run = run + pltpu.roll(run, n - w // 2, axis=0)
        centred.append(pltpu.roll(run, w // 2, axis=0)[POOL_HALO:POOL_HALO + TP])
    lane = lax.broadcasted_iota(jnp.int32, (TP, width), 1)
    grp = lane // (width // len(POOL_WINDOWS))
    win = centred[-1]
    half = jnp.full((TP, width), POOL_WINDOWS[-1] // 2, jnp.int32)
    for gi in range(len(POOL_WINDOWS) - 2, -1, -1):
        win = jnp.where(grp == gi, centred[gi], win)
        half = jnp.where(grp == gi, POOL_WINDOWS[gi] // 2, half)
    t = c * TP + lax.broadcasted_iota(jnp.int32, (TP, width), 0)
    cnt = (jnp.clip(t + half, 0, S) - jnp.clip(t - half, 0, S)).astype(F32)
    y = win / cnt - xs[POOL_HALO:POOL_HALO + TP]
    o_ref[0] = (_dot(y.astype(BF16), w_ref[...]) * sc_ref[...]).astype(BF16)


def _pool(p_pool, w_bd, scale):
    B, S, W = p_pool.shape
    tp = min(256, S)
    xp = jnp.pad(p_pool, ((0, 0), (POOL_HALO, POOL_HALO), (0, 0)))
    return pl.pallas_call(
        functools.partial(_pool_kernel, S=S, TP=tp),
        grid=(B, S // tp),
        in_specs=[pl.BlockSpec((1, S + 2 * POOL_HALO, W), lambda b, c: (b, 0, 0)),
                  pl.BlockSpec((W, W), lambda b, c: (0, 0)),
                  pl.BlockSpec((1, W), lambda b, c: (0, 0))],
        out_specs=pl.BlockSpec((1, tp, W), lambda b, c: (b, c, 0)),
        out_shape=jax.ShapeDtypeStruct((B, S, W), BF16),
        compiler_params=_params(("parallel", "arbitrary")),
        name="pool_mixer",
    )(xp, w_bd, scale)


def _dft_kernel(ca_ref, sa_ref, cb_ref, sb_ref, c_ref, s_ref):
    ca, sa = ca_ref[0], sa_ref[0]
    cb, sb = cb_ref[...], sb_ref[...]
    c_ref[...] = (ca * cb - sa * sb).astype(BF16)
    s_ref[...] = (sa * cb + ca * sb).astype(BF16)


def _dft_tables(S):
    R = min(DFT_ROWS, S)
    k = np.arange(S, dtype=np.int64)
    a = np.arange(S // R, dtype=np.int64)[:, None]
    b = np.arange(R, dtype=np.int64)[:, None]
    ang_a = 2.0 * np.pi * ((a * R * k) % S) / S
    ang_b = 2.0 * np.pi * ((b * k) % S) / S
    ca = jnp.asarray(np.cos(ang_a), F32).reshape(S // R, 1, S)
    sa = jnp.asarray(np.sin(ang_a), F32).reshape(S // R, 1, S)
    cb = jnp.asarray(np.cos(ang_b), F32)
    sb = jnp.asarray(np.sin(ang_b), F32)
    row = pl.BlockSpec((1, 1, S), lambda i: (i, 0, 0))
    full = pl.BlockSpec((R, S), lambda i: (0, 0))
    out = pl.BlockSpec((R, S), lambda i: (i, 0))
    return pl.pallas_call(
        _dft_kernel,
        grid=(S // R,),
        in_specs=[row, row, full, full],
        out_specs=[out, out],
        out_shape=[jax.ShapeDtypeStruct((S, S), BF16)] * 2,
        compiler_params=_params(("parallel",)),
        name="dft_tables",
    )(ca, sa, cb, sb)


def _channel_dft(FW):
    gw = FW // FOURIER_GROUPS
    k = np.arange(gw)
    ang = 2.0 * np.pi * ((k[:, None] * k[None, :]) % gw) / gw
    cs = np.zeros((FW, 2 * FW), np.float64)
    for g in range(FOURIER_GROUPS):
        cs[g * gw:(g + 1) * gw, g * gw:(g + 1) * gw] = np.cos(ang)
        cs[g * gw:(g + 1) * gw, FW + g * gw:FW + (g + 1) * gw] = np.sin(ang)
    return jnp.asarray(cs, BF16)


def _four_kernel(c_ref, s_ref, xcs_ref, wf_ref, o_ref, *, FW, scale):
    y = _dot(c_ref[...], xcs_ref[0, :, :FW]) - _dot(s_ref[...], xcs_ref[0, :, FW:])
    o_ref[0] = _dot((y * scale).astype(BF16), wf_ref[...]).astype(BF16)


def _fourier(xcs, tabs, wf_bd):
    B, S, FW2 = xcs.shape
    FW = FW2 // 2
    tq = min(512, S)
    scale = 1.0 / math.sqrt(S * (FW // FOURIER_GROUPS))
    return pl.pallas_call(
        functools.partial(_four_kernel, FW=FW, scale=scale),
        grid=(S // tq, B),
        in_specs=[pl.BlockSpec((tq, S), lambda i, b: (i, 0)),
                  pl.BlockSpec((tq, S), lambda i, b: (i, 0)),
                  pl.BlockSpec((1, S, FW2), lambda i, b: (b, 0, 0)),
                  pl.BlockSpec((FW, FW), lambda i, b: (0, 0))],
        out_specs=pl.BlockSpec((1, tq, FW), lambda i, b: (b, i, 0)),
        out_shape=jax.ShapeDtypeStruct((B, S, FW), BF16),
        compiler_params=_params(("parallel", "arbitrary")),
        name="fourier_mixer",
    )(tabs[0], tabs[1], xcs, wf_bd)


def _softmax_pv(s, sink, v):
    m = jnp.maximum(jnp.max(s, axis=-1, keepdims=True), sink)
    p = jnp.exp(s - m)
    den = jnp.sum(p, axis=-1, keepdims=True) + jnp.exp(sink - m)
    return _dot(p.astype(BF16), v) / den


def _attn_ctx_kernel(sink_ref, q_ref, k_ref, v_ref, o_ref, *, G):
    q = q_ref[0]
    k = k_ref[0].astype(BF16)
    v = v_ref[0].astype(BF16)
    for kh in range(N_KV_HEADS):
        kk = k[:, kh * HEAD_DIM:(kh + 1) * HEAD_DIM]
        vv = v[:, kh * HEAD_DIM:(kh + 1) * HEAD_DIM]
        for g in range(G):
            hd = kh * G + g
            s = _dot_nt(q[:, hd * HEAD_DIM:(hd + 1) * HEAD_DIM], kk)
            o = _softmax_pv(s, sink_ref[hd], vv)
            o_ref[0, :, hd * HEAD_DIM:(hd + 1) * HEAD_DIM] = o.astype(BF16)


def _attn_ctx(sink, q, k, v):
    B, S, AW = q.shape
    KW = k.shape[2]
    G = AW // HEAD_DIM // N_KV_HEADS
    return pl.pallas_call(
        functools.partial(_attn_ctx_kernel, G=G),
        grid=(B,),
        in_specs=[pl.BlockSpec(memory_space=pltpu.SMEM),
                  pl.BlockSpec((1, S, AW), lambda b: (b, 0, 0)),
                  pl.BlockSpec((1, S, KW), lambda b: (b, 0, 0)),
                  pl.BlockSpec((1, S, KW), lambda b: (b, 0, 0))],
        out_specs=pl.BlockSpec((1, S, AW), lambda b: (b, 0, 0)),
        out_shape=jax.ShapeDtypeStruct((B, S, AW), BF16),
        compiler_params=_params(("parallel",)),
        name="attn_context",
    )(sink, q, k, v)


def _attn_lat_kernel(sink_ref, q_ref, kp_ref, kc_ref, kn_ref, vp_ref, vc_ref, vn_ref, ck_ref, cv_ref, o_ref,
                     *, S, G):
    i = pl.program_id(1)
    blk = ATTN_BLOCK
    q = q_ref[0]
    kcat = jnp.concatenate([kp_ref[0], kc_ref[0], kn_ref[0], ck_ref[0]], axis=0).astype(BF16)
    vcat = jnp.concatenate([vp_ref[0], vc_ref[0], vn_ref[0], cv_ref[0]], axis=0).astype(BF16)
    nk = kcat.shape[0]
    qi = lax.broadcasted_iota(jnp.int32, (blk, nk), 0)
    kj = lax.broadcasted_iota(jnp.int32, (blk, nk), 1)
    pos = i * blk - blk + kj
    local_ok = jnp.where(jnp.abs(kj - blk - qi) <= WINDOW,
                         jnp.where(pos >= 0, jnp.where(pos < S, 1, 0), 0), 0)
    ok = jnp.where(kj >= 3 * blk, 1, local_ok) > 0
    ok = jnp.concatenate([ok] * G, axis=0)
    for kh in range(N_KV_HEADS):
        kk = kcat[:, kh * HEAD_DIM:(kh + 1) * HEAD_DIM]
        vv = vcat[:, kh * HEAD_DIM:(kh + 1) * HEAD_DIM]
        qs = jnp.concatenate([q[:, (kh * G + g) * HEAD_DIM:(kh * G + g + 1) * HEAD_DIM] for g in range(G)], axis=0)
        sink = jnp.concatenate([jnp.full((blk, 1), sink_ref[kh * G + g], F32) for g in range(G)], axis=0)
        s = jnp.where(ok, _dot_nt(qs, kk), NEG_INF)
        o = _softmax_pv(s, sink, vv)
        for g in range(G):
            hd = kh * G + g
            o_ref[0, :, hd * HEAD_DIM:(hd + 1) * HEAD_DIM] = o[g * blk:(g + 1) * blk].astype(BF16)


def _attn_lat(sink, q, k, v, ck, cv):
    B, S, AW = q.shape
    KW = k.shape[2]
    C = ck.shape[1]
    G = AW // HEAD_DIM // N_KV_HEADS
    nb = S // ATTN_BLOCK
    prev = pl.BlockSpec((1, ATTN_BLOCK, KW), lambda b, i: (b, jnp.maximum(i - 1, 0), 0))
    cur = pl.BlockSpec((1, ATTN_BLOCK, KW), lambda b, i: (b, i, 0))
    nxt = pl.BlockSpec((1, ATTN_BLOCK, KW), lambda b, i: (b, jnp.minimum(i + 1, nb - 1), 0))
    ctx = pl.BlockSpec((1, C, KW), lambda b, i: (b, 0, 0))
    return pl.pallas_call(
        functools.partial(_attn_lat_kernel, S=S, G=G),
        grid=(B, nb),
        in_specs=[pl.BlockSpec(memory_space=pltpu.SMEM),
                  pl.BlockSpec((1, ATTN_BLOCK, AW), lambda b, i: (b, i, 0)),
                  prev, cur, nxt, prev, cur, nxt, ctx, ctx],
        out_specs=pl.BlockSpec((1, ATTN_BLOCK, AW), lambda b, i: (b, i, 0)),
        out_shape=jax.ShapeDtypeStruct((B, S, AW), BF16),
        compiler_params=_params(("parallel", "parallel")),
        name="attn_latent",
    )(sink, q, k, k, k, v, v, v, ck, cv)


def _out_proj_kernel(pool_ref, four_ref, attn_ref, x_ref, mod_ref, ln_ref, w_ref, x1_ref, h2t_ref, *, D, PW, FW):
    mixed = (_dot(pool_ref[...], w_ref[0:PW, :]) + _dot(four_ref[...], w_ref[PW:PW + FW, :])
             + _dot(attn_ref[...], w_ref[PW + FW:, :]))
    mod = mod_ref[0]
    g1 = mod[:, 2 * D:3 * D]
    sh2 = mod[:, 3 * D:4 * D]
    sc2 = mod[:, 4 * D:5 * D]
    x1 = x_ref[...] + g1 * mixed
    h2 = _rms(x1, ln_ref[...]) * (1.0 + sc2) + sh2
    x1_ref[...] = x1
    h2t_ref[...] = h2.T.astype(BF16)


def _out_proj(pool_o, four_o, attn_o, x, mod3, ln, w_out, seq_len, per_seq):
    T, D = x.shape
    PW, FW, AW = pool_o.shape[1], four_o.shape[1], attn_o.shape[1]
    tm = min(256, seq_len)
    return pl.pallas_call(
        functools.partial(_out_proj_kernel, D=D, PW=PW, FW=FW),
        grid=(T // tm,),
        in_specs=[pl.BlockSpec((tm, PW), lambda i: (i, 0)),
                  pl.BlockSpec((tm, FW), lambda i: (i, 0)),
                  pl.BlockSpec((tm, AW), lambda i: (i, 0)),
                  pl.BlockSpec((tm, D), lambda i: (i, 0)),
                  _mod_spec(6 * D, tm, seq_len, per_seq),
                  pl.BlockSpec((1, D), lambda i: (0, 0)),
                  pl.BlockSpec((PW + FW + AW, D), lambda i: (0, 0))],
        out_specs=[pl.BlockSpec((tm, D), lambda i: (i, 0)),
                   pl.BlockSpec((D, tm), lambda i: (0, i))],
        out_shape=[jax.ShapeDtypeStruct((T, D), F32), jax.ShapeDtypeStruct((D, T), BF16)],
        compiler_params=_params(("parallel",)),
        name="out_proj",
    )(pool_o, four_o, attn_o, x, mod3, ln, w_out)


def _top_values(vals, k, ranked=False):
    out = []
    rank = jnp.full(vals.shape, float(vals.shape[0]), F32) if ranked else None
    for r in range(k):
        m = jnp.max(vals, axis=0, keepdims=True)
        out.append(m)
        hit = vals == m
        if ranked:
            rank = jnp.where(hit, float(r), rank)
        if r + 1 < k:
            vals = jnp.where(hit, -jnp.inf, vals)
    return (out, rank) if ranked else out


def _peer_scores_kernel(xt_ref, wqt_ref, sk_ref, a1_ref, cnt_ref, a2_ref, r2_ref, qt_ref, cand_ref, *, H, K):
    half = sk_ref.shape[3]
    qt_ref[...] = _dot(wqt_ref[...], xt_ref[...]).astype(BF16)
    pairs = [(i, j) for i in range(K) for j in range(K) if (i + 1) * (j + 1) <= K]
    cand_ref[...] = jnp.full(cand_ref.shape, -jnp.inf, F32)
    for h in range(H):
        s1 = _dot(sk_ref[h, 0], qt_ref[(2 * h) * half:(2 * h + 1) * half, :])
        s2 = _dot(sk_ref[h, 1], qt_ref[(2 * h + 1) * half:(2 * h + 2) * half, :])
        v1 = _top_values(s1, K)
        v2, rank2 = _top_values(s2, K, ranked=True)
        sums = {}
        for n, (i, j) in enumerate(pairs):
            sums[(i, j)] = v1[i] + v2[j]
            cand_ref[n:n + 1, :] = sums[(i, j)]
        best = _top_values(cand_ref[...], K)
        theta, top = best[K - 1], best[0]
        z = jnp.exp(best[0] - top)
        for r in range(1, K):
            z = z + jnp.exp(best[r] - top)
        cnt = jnp.zeros(s1.shape, F32)
        for i in range(K):
            ci = jnp.zeros(theta.shape, F32)
            for j in range(K):
                if (i, j) in sums:
                    ci = ci + jnp.where(sums[(i, j)] >= theta, 1.0, 0.0)
            cnt = jnp.where(s1 == v1[i], ci, cnt)
        a1_ref[:, h, :] = jnp.exp(s1 - v1[0]) * (1.0 / z)
        cnt_ref[:, h, :] = cnt
        a2 = jnp.exp(s2 - v2[0]).astype(BF16)
        rank2 = rank2.astype(BF16)
        for lc in range(s2.shape[1] // 128):
            ls = slice(lc * 128, (lc + 1) * 128)
            a2_ref[lc, h] = pltpu.bitcast(a2[:, ls], jnp.int32)
            r2_ref[lc, h] = pltpu.bitcast(rank2[:, ls], jnp.int32)


def _peer_scores(h2t, wqt, sk):
    D, T = h2t.shape
    H, _, NK, half = sk.shape
    tn = min(256, T)
    n_pairs = sum(1 for i in range(PEER_TOPK) for j in range(PEER_TOPK) if (i + 1) * (j + 1) <= PEER_TOPK)
    rows = pl.BlockSpec((NK, H, tn), lambda i: (0, 0, i))
    packed = pl.BlockSpec((tn // 128, H, NK // 2, 128), lambda i: (i, 0, 0, 0))
    return pl.pallas_call(
        functools.partial(_peer_scores_kernel, H=H, K=PEER_TOPK),
        grid=(T // tn,),
        in_specs=[pl.BlockSpec((D, tn), lambda i: (0, i)),
                  pl.BlockSpec(wqt.shape, lambda i: (0, 0)),
                  pl.BlockSpec(sk.shape, lambda i: (0, 0, 0, 0))],
        out_specs=[rows, rows, packed, packed],
        out_shape=[jax.ShapeDtypeStruct((NK, H, T), F32)] * 2 + [jax.ShapeDtypeStruct((T // 128, H, NK // 2, 128), jnp.int32)] * 2,
        scratch_shapes=[pltpu.VMEM((wqt.shape[0], tn), BF16),
                        pltpu.VMEM((-(-n_pairs // 8) * 8, tn), F32)],
        compiler_params=_params(("parallel",)),
        name="peer_scores",
    )(h2t, wqt, sk)


def _gelu_tanh(x):
    return x * (0.5 * (1.0 + jnp.tanh(math.sqrt(2.0 / math.pi) * (x + 0.044715 * (x * x * x)))))


def _zero_of(v):
    u = pltpu.bitcast(v, jnp.uint32)
    return lax.shift_right_logical(lax.shift_right_logical(u, jnp.uint32(16)), jnp.uint32(16))


def _tied(w, zero):
    if zero is None:
        return w
    k, n = w.shape
    parts = []
    for kb in range(0, k, MXU_DEPTH):
        head = pltpu.bitcast(w[kb:kb + 16, :], jnp.uint32) + jnp.tile(zero, (1, n // 128))
        parts += [pltpu.bitcast(head, BF16), w[kb + 16:kb + MXU_DEPTH, :]]
    return jnp.concatenate(parts, axis=0)


def _peer_dense_kernel(*refs, D, H, NK, R, NJ, final):
    if final:
        (xt_ref, u_ref, vt_ref, a1_ref, cnt_ref, a2_ref, r2_ref, x1_ref, mod_ref, fn_ref, o_ref,
         acc_ref, act0_ref, act1_ref, p0_ref, p1_ref) = refs
    else:
        (xt_ref, u_ref, vt_ref, a1_ref, cnt_ref, a2_ref, r2_ref, x1_ref, mod_ref, o_ref,
         acc_ref, act0_ref, act1_ref, p0_ref, p1_ref) = refs
    s = pl.program_id(0)
    tn = xt_ref.shape[1]

    @pl.when(s == 0)
    def _():
        acc_ref[...] = jnp.zeros(acc_ref.shape, F32)
        act1_ref[...] = jnp.zeros(act1_ref.shape, BF16)
        p0_ref[...] = jnp.zeros(p0_ref.shape, BF16)

    ec = R * NK
    n_lc = tn // 128

    def gate_block(r, lc, act_b, p_b):
        ls = slice(lc * 128, (lc + 1) * 128)
        rows = slice(r * NK, (r + 1) * NK)
        a1_heads = a1_ref[r, :, ls]
        cnt_heads = cnt_ref[r, :, ls]
        gate = None
        for h in range(H):
            a1 = jnp.broadcast_to(a1_heads[h:h + 1, :], (NK, 128)).astype(BF16)
            cnt = jnp.broadcast_to(cnt_heads[h:h + 1, :], (NK, 128)).astype(BF16)
            rank2 = pltpu.bitcast(r2_ref[lc, h], BF16)
            a2 = pltpu.bitcast(a2_ref[lc, h], BF16)
            term = a1 * jnp.where(rank2 < cnt, a2, jnp.zeros((), BF16))
            gate = term if gate is None else gate + term
        p = gate * act_b[lc, rows, :]
        p_b[lc, rows, :] = p
        return _zero_of(p[0:16, :])

    def stages(act_a, act_b, p_b, p_c):
        blocks = [(r, lc) for r in range(R) for lc in range(n_lc)]
        per_group = len(blocks) // (2 * PEER_PIECES)
        groups = [blocks[g * per_group:(g + 1) * per_group] for g in range(2 * PEER_PIECES)]

        def gate_group(g):
            zero = None
            for r, lc in groups[g]:
                z = gate_block(r, lc, act_b, p_b)
                zero = z if zero is None else zero | z
            return zero

        xt = xt_ref[...]
        pc = jnp.concatenate([p_c[lc] for lc in range(n_lc)], axis=1)
        tie = None
        for q in range(PEER_PIECES):
            ra = slice(q * (ec // PEER_PIECES), (q + 1) * (ec // PEER_PIECES))
            hq = _dot(u_ref[ra, :], _tied(xt, tie))
            tie = gate_group(2 * q)
            for r0 in range(ra.start, ra.stop, GELU_ROWS):
                for lc in range(n_lc):
                    ls = slice(lc * 128, (lc + 1) * 128)
                    act_a[lc, r0:r0 + GELU_ROWS, :] = _gelu_tanh(
                        hq[r0 - ra.start:r0 - ra.start + GELU_ROWS, ls].astype(BF16))
            rc = slice(q * (D // PEER_PIECES), (q + 1) * (D // PEER_PIECES))
            acc_ref[rc, :] += _dot(vt_ref[rc, :], _tied(pc, tie))
            tie = gate_group(2 * q + 1)

    @pl.when(s % 2 == 0)
    def _():
        stages(act0_ref, act1_ref, p1_ref, p0_ref)

    @pl.when(s % 2 == 1)
    def _():
        stages(act1_ref, act0_ref, p0_ref, p1_ref)

    c = s - 2

    @pl.when(jnp.logical_and(c >= 0, c % NJ == NJ - 1))
    def _():
        g2 = mod_ref[0][:, 5 * D:6 * D]
        x2 = x1_ref[...] + g2 * acc_ref[...].T
        o_ref[...] = _rms(x2, fn_ref[...]) if final else x2
        acc_ref[...] = jnp.zeros(acc_ref.shape, F32)


def _peer_dense(h2t, u, vt, a1, cnt, a2, r2, x1, mod3, final_norm, seq_len, per_seq):
    D, T = h2t.shape
    E = u.shape[0]
    NK, H, _ = a1.shape
    R = 8
    ec = R * NK
    tn = min(512, seq_len if per_seq else T)
    final = final_norm is not None
    nj = E // ec
    total = (T // tn) * nj
    tiles_per_seq = max(seq_len // tn, 1)

    def item(lag):
        return lambda s: jnp.clip(s - lag, 0, total - 1)

    ia, ib, ic = item(0), item(1), item(2)
    mod_row = (lambda s: 1 + (ic(s) // nj) // tiles_per_seq) if per_seq else (lambda s: 0)
    tab = pl.BlockSpec((tn // 128, H, NK // 2, 128), lambda s: (ib(s) // nj, 0, 0, 0))
    row = pl.BlockSpec((R, H, tn), lambda s: (ib(s) % nj, 0, ib(s) // nj))
    in_specs = [pl.BlockSpec((D, tn), lambda s: (0, ia(s) // nj)),
                pl.BlockSpec((ec, D), lambda s: (ia(s) % nj, 0)),
                pl.BlockSpec((D, ec), lambda s: (0, ic(s) % nj)),
                row, row, tab, tab,
                pl.BlockSpec((tn, D), lambda s: (ic(s) // nj, 0)),
                pl.BlockSpec((1, 1, 6 * D), lambda s: (mod_row(s), 0, 0))]
    args = [h2t, u, vt, a1, cnt, a2, r2, x1, mod3]
    if final:
        in_specs.append(pl.BlockSpec((1, D), lambda s: (0, 0)))
        args.append(final_norm)
    return pl.pallas_call(
        functools.partial(_peer_dense_kernel, D=D, H=H, NK=NK, R=R, NJ=nj, final=final),
        grid=(total + 2,),
        in_specs=in_specs,
        out_specs=pl.BlockSpec((tn, D), lambda s: (ic(s) // nj, 0)),
        out_shape=jax.ShapeDtypeStruct((T, D), F32),
        scratch_shapes=[pltpu.VMEM((D, tn), F32)] + [pltpu.VMEM((tn // 128, ec, 128), BF16)] * 4,
        compiler_params=_params(("arbitrary",)),
        name="peer_dense_final" if final else "peer_dense",
    )(*args)


def _rope_tables(S):
    pos = np.arange(S)
    inv_freq = ROPE_THETA ** (-(np.arange(ROPE_HALF // 2, dtype=np.float64) * 2.0 / ROPE_HALF))
    d = np.arange(HEAD_DIM)
    p = np.where(d[None, :] < ROPE_HALF, (pos // GRID_W)[:, None], (pos % GRID_W)[:, None])
    ang = p * inv_freq[d % (ROPE_HALF // 2)][None, :]
    sign = np.where((d % ROPE_HALF) < ROPE_HALF // 2, -1.0, 1.0)[None, :]
    cos = np.tile(np.cos(ang), (1, 128 // HEAD_DIM))
    sin = np.tile(np.sin(ang) * sign, (1, 128 // HEAD_DIM))
    return jnp.asarray(cos, F32), jnp.asarray(sin, F32)


def _block_diag(w):
    g, a, b = w.shape
    out = jnp.zeros((g * a, g * b), w.dtype)
    for i in range(g):
        out = out.at[i * a:(i + 1) * a, i * b:(i + 1) * b].set(w[i])
    return out


def kernel(x_prompt, x_sample, cache_k, cache_v, c, c_ctx, w_ada, b_ada, norm_mix, norm_ffn, w_in, w_pool,
           pool_scale, w_fourier, attn_sink, w_out, peer_w_query, peer_sub_keys, peer_u, peer_v, final_norm):
    B, S, D = x_prompt.shape
    BD, SD, _ = x_sample.shape
    L = w_ada.shape[0]
    C = cache_k.shape[2]
    PW = w_pool.shape[1] * w_pool.shape[2]
    FW = w_fourier.shape[1] * w_fourier.shape[2]
    KW = cache_k.shape[3] * cache_k.shape[4]
    AW = w_in.shape[2] - PW - FW - 2 * KW
    dims = (PW, FW, AW, KW)
    assert BD + 1 <= MOD_ROWS and SD % ATTN_BLOCK == 0 and SD % GRID_W == 0

    cvec = jnp.concatenate([c_ctx[None, :], c, jnp.zeros((MOD_ROWS - 1 - BD, D), F32)], axis=0)
    mod = _ada(cvec, w_ada, b_ada).reshape(L, MOD_ROWS, 1, 6 * D)
    cs = _channel_dft(FW)
    tabs_p = _dft_tables(S)
    tabs_s = _dft_tables(SD)
    rope = _rope_tables(SD)

    xp = x_prompt.reshape(B * S, D)
    xs = x_sample.reshape(BD * SD, D)
    new_k, new_v = [], []
    for l in range(L):
        last = l == L - 1
        w_in_l = w_in[l].astype(BF16)
        w_pool_l = _block_diag(w_pool[l]).astype(BF16)
        w_four_l = _block_diag(w_fourier[l]).astype(BF16)
        w_out_l = w_out[l].astype(BF16)
        wqt_l = peer_w_query[l].T.astype(BF16)
        sk_l = peer_sub_keys[l].astype(BF16)
        u_l = peer_u[l].astype(BF16)
        vt_l = peer_v[l].T.astype(BF16)
        ln1 = norm_mix[l][None, :]
        ln2 = norm_ffn[l][None, :]
        scale_l = pool_scale[l][None, :]
        sink_l = attn_sink[l]
        fn = final_norm[None, :] if last else None
        ck = cache_k[:, l].reshape(BD, C, KW)
        cv = cache_v[:, l].reshape(BD, C, KW)

        def layer(x, nb, seq, per_seq, rope_tabs, tabs):
            pool_in, xcs, q, k, v = _in_proj(x, mod[l], ln1, w_in_l, cs, rope_tabs, seq, per_seq, dims)
            pool_o = _pool(pool_in.reshape(nb, seq, PW), w_pool_l, scale_l)
            four_o = _fourier(xcs.reshape(nb, seq, 2 * FW), tabs, w_four_l)
            q3, k3, v3 = q.reshape(nb, seq, AW), k.reshape(nb, seq, KW), v.reshape(nb, seq, KW)
            if per_seq:
                attn_o = _attn_lat(sink_l, q3, k3, v3, ck, cv)
            else:
                attn_o = _attn_ctx(sink_l, q3, k3, v3)
            x1, h2t = _out_proj(pool_o.reshape(-1, PW), four_o.reshape(-1, FW), attn_o.reshape(-1, AW),
                                x, mod[l], ln2, w_out_l, seq, per_seq)
            a1, cnt, a2, r2 = _peer_scores(h2t, wqt_l, sk_l)
            x2 = _peer_dense(h2t, u_l, vt_l, a1, cnt, a2, r2, x1, mod[l], fn, seq, per_seq)
            return x2, k3, v3

        xp, k_l, v_l = layer(xp, B, S, False, None, tabs_p)
        xs, _, _ = layer(xs, BD, SD, True, rope, tabs_s)
        new_k.append(k_l.reshape(B, S, N_KV_HEADS, HEAD_DIM))
        new_v.append(v_l.reshape(B, S, N_KV_HEADS, HEAD_DIM))
    return (xp.reshape(B, S, D), xs.reshape(BD, SD, D), jnp.stack(new_k, axis=1), jnp.stack(new_v, axis=1))
```

```python
import functools
import math

import numpy as np
import jax
import jax.numpy as jnp
from jax import lax
from jax.experimental import pallas as pl
from jax.experimental.pallas import tpu as pltpu

F32 = jnp.float32
BF16 = jnp.bfloat16

HEAD_DIM = 64
N_KV_HEADS = 2
POOL_WINDOWS = (2, 4, 8, 16)
POOL_HALO = max(POOL_WINDOWS) // 2
FOURIER_GROUPS = 4
GRID_W = 64
WINDOW = 128
ATTN_BLOCK = 128
ROPE_THETA = 10000.0
ROPE_HALF = HEAD_DIM // 2
PEER_TOPK = 16
NORM_EPS = 1e-6
NEG_INF = -1e30
DFT_ROWS = 64
MOD_ROWS = 16
V7X_VMEM_LIMIT = 48 * 1024 * 1024
MXU_DEPTH = 256
PEER_CHUNK_KEYS = 8
PEER_PIECES = 8
GELU_ROWS = 32


def _params(semantics):
    return pltpu.CompilerParams(dimension_semantics=semantics, vmem_limit_bytes=V7X_VMEM_LIMIT)


def _dot(a, b):
    return jnp.dot(a, b, preferred_element_type=F32)


def _dot_nt(a, b):
    return lax.dot_general(a, b, (((1,), (1,)), ((), ())), preferred_element_type=F32)


def _rms(x, g):
    return x * lax.rsqrt(jnp.mean(x * x, axis=-1, keepdims=True) + NORM_EPS) * g


def _ada_kernel(c_ref, w_ref, b_ref, o_ref):
    cv = c_ref[...]
    act = cv * (1.0 / (1.0 + jnp.exp(-cv)))
    o_ref[0] = _dot(act.astype(BF16), w_ref[0].astype(BF16)) + b_ref[0]


def _ada(cvec, w_ada, b_ada):
    L, D, N = w_ada.shape
    tn = N // 4
    return pl.pallas_call(
        _ada_kernel,
        grid=(L, N // tn),
        in_specs=[pl.BlockSpec((MOD_ROWS, D), lambda l, j: (0, 0)),
                  pl.BlockSpec((1, D, tn), lambda l, j: (l, 0, j)),
                  pl.BlockSpec((1, 1, tn), lambda l, j: (l, 0, j))],
        out_specs=pl.BlockSpec((1, MOD_ROWS, tn), lambda l, j: (l, 0, j)),
        out_shape=jax.ShapeDtypeStruct((L, MOD_ROWS, N), F32),
        compiler_params=_params(("parallel", "parallel")),
        name="ada_mod",
    )(cvec, w_ada, b_ada.reshape(L, 1, N))


def _rope_apply(x, cos, sin):
    lane = lax.broadcasted_iota(jnp.int32, (x.shape[0], 128), 1)
    first = (lane % ROPE_HALF) < (ROPE_HALF // 2)
    outs = []
    for g in range(x.shape[1] // 128):
        xg = x[:, g * 128:(g + 1) * 128]
        partner = jnp.where(first, pltpu.roll(xg, 128 - ROPE_HALF // 2, axis=1),
                            pltpu.roll(xg, ROPE_HALF // 2, axis=1))
        outs.append(xg * cos + partner * sin)
    return outs[0] if len(outs) == 1 else jnp.concatenate(outs, axis=1)


def _in_proj_kernel(*refs, D, PW, FW, AW, KW, rope):
    if rope:
        x_ref, mod_ref, ln_ref, w_ref, cs_ref, cos_ref, sin_ref, pool_ref, xcs_ref, q_ref, k_ref, v_ref = refs
    else:
        x_ref, mod_ref, ln_ref, w_ref, cs_ref, pool_ref, xcs_ref, q_ref, k_ref, v_ref = refs
    mod = mod_ref[0]
    sh1 = mod[:, 0:D]
    sc1 = mod[:, D:2 * D]
    h = _rms(x_ref[...], ln_ref[...]) * (1.0 + sc1) + sh1
    proj = _dot(h.astype(BF16), w_ref[...])
    o1, o2 = PW, PW + FW
    o3 = o2 + AW
    o4 = o3 + KW
    pool_ref[...] = proj[:, :o1]
    xcs_ref[...] = _dot(proj[:, o1:o2].astype(BF16), cs_ref[...]).astype(BF16)
    q = proj[:, o2:o3]
    k = proj[:, o3:o4]
    if rope:
        q = _rope_apply(q, cos_ref[...], sin_ref[...])
        k = _rope_apply(k, cos_ref[...], sin_ref[...])
    q_ref[...] = (q * (HEAD_DIM ** -0.5)).astype(BF16)
    k_ref[...] = k
    v_ref[...] = proj[:, o4:]


def _mod_spec(n_cols, tile, seq_len, per_seq):
    tiles_per_seq = max(seq_len // tile, 1)
    if per_seq:
        return pl.BlockSpec((1, 1, n_cols), lambda i, *_: (1 + i // tiles_per_seq, 0, 0))
    return pl.BlockSpec((1, 1, n_cols), lambda i, *_: (0, 0, 0))


def _in_proj(x, mod3, ln, w_in, cs, rope_tabs, seq_len, per_seq, dims):
    T, D = x.shape
    PW, FW, AW, KW = dims
    tm = min(256, seq_len)
    rope = rope_tabs is not None
    n_in = w_in.shape[1]
    tiles_per_seq = seq_len // tm
    in_specs = [pl.BlockSpec((tm, D), lambda i: (i, 0)),
                _mod_spec(6 * D, tm, seq_len, per_seq),
                pl.BlockSpec((1, D), lambda i: (0, 0)),
                pl.BlockSpec((D, n_in), lambda i: (0, 0)),
                pl.BlockSpec((FW, 2 * FW), lambda i: (0, 0))]
    args = [x, mod3, ln, w_in, cs]
    if rope:
        in_specs += [pl.BlockSpec((tm, 128), lambda i: (i % tiles_per_seq, 0))] * 2
        args += list(rope_tabs)
    outs = [(PW, F32), (2 * FW, BF16), (AW, BF16), (KW, F32), (KW, F32)]
    return pl.pallas_call(
        functools.partial(_in_proj_kernel, D=D, PW=PW, FW=FW, AW=AW, KW=KW, rope=rope),
        grid=(T // tm,),
        in_specs=in_specs,
        out_specs=[pl.BlockSpec((tm, w), lambda i: (i, 0)) for w, _ in outs],
        out_shape=[jax.ShapeDtypeStruct((T, w), dt) for w, dt in outs],
        compiler_params=_params(("parallel",)),
        name="in_proj_rope" if rope else "in_proj",
    )(*args)


def _pool_kernel(xp_ref, w_ref, sc_ref, o_ref, *, S, TP):
    c = pl.program_id(1)
    n = TP + 2 * POOL_HALO
    xs = xp_ref[0, pl.ds(pl.multiple_of(c * TP, 8), n), :]
    width = xs.shape[1]
    run = xs
    centred = []
    for w in POOL_WINDOWS:
        run = run + pltpu.roll(run, n - w // 2, axis=0)
        centred.append(pltpu.roll(run, w // 2, axis=0)[POOL_HALO:POOL_HALO + TP])
    lane = lax.broadcasted_iota(jnp.int32, (TP, width), 1)
    grp = lane // (width // len(POOL_WINDOWS))
    win = centred[-1]
    half = jnp.full((TP, width), POOL_WINDOWS[-1] // 2, jnp.int32)
    for gi in range(len(POOL_WINDOWS) - 2, -1, -1):
        win = jnp.where(grp == gi, centred[gi], win)
        half = jnp.where(grp == gi, POOL_WINDOWS[gi] // 2, half)
    t = c * TP + lax.broadcasted_iota(jnp.int32, (TP, width), 0)
    cnt = (jnp.clip(t + half, 0, S) - jnp.clip(t - half, 0, S)).astype(F32)
    y = win / cnt - xs[POOL_HALO:POOL_HALO + TP]
    o_ref[0] = (_dot(y.astype(BF16), w_ref[...]) * sc_ref[...]).astype(BF16)


def _pool(p_pool, w_bd, scale):
    B, S, W = p_pool.shape
    tp = min(256, S)
    xp = jnp.pad(p_pool, ((0, 0), (POOL_HALO, POOL_HALO), (0, 0)))
    return pl.pallas_call(
        functools.partial(_pool_kernel, S=S, TP=tp),
        grid=(B, S // tp),
        in_specs=[pl.BlockSpec((1, S + 2 * POOL_HALO, W), lambda b, c: (b, 0, 0)),
                  pl.BlockSpec((W, W), lambda b, c: (0, 0)),
                  pl.BlockSpec((1, W), lambda b, c: (0, 0))],
        out_specs=pl.BlockSpec((1, tp, W), lambda b, c: (b, c, 0)),
        out_shape=jax.ShapeDtypeStruct((B, S, W), BF16),
        compiler_params=_params(("parallel", "arbitrary")),
        name="pool_mixer",
    )(xp, w_bd, scale)


def _dft_kernel(ca_ref, sa_ref, cb_ref, sb_ref, c_ref, s_ref):
    ca, sa = ca_ref[0], sa_ref[0]
    cb, sb = cb_ref[...], sb_ref[...]
    c_ref[...] = (ca * cb - sa * sb).astype(BF16)
    s_ref[...] = (sa * cb + ca * sb).astype(BF16)


def _dft_tables(S):
    R = min(DFT_ROWS, S)
    k = np.arange(S, dtype=np.int64)
    a = np.arange(S // R, dtype=np.int64)[:, None]
    b = np.arange(R, dtype=np.int64)[:, None]
    ang_a = 2.0 * np.pi * ((a * R * k) % S) / S
    ang_b = 2.0 * np.pi * ((b * k) % S) / S
    ca = jnp.asarray(np.cos(ang_a), F32).reshape(S // R, 1, S)
    sa = jnp.asarray(np.sin(ang_a), F32).reshape(S // R, 1, S)
    cb = jnp.asarray(np.cos(ang_b), F32)
    sb = jnp.asarray(np.sin(ang_b), F32)
    row = pl.BlockSpec((1, 1, S), lambda i: (i, 0, 0))
    full = pl.BlockSpec((R, S), lambda i: (0, 0))
    out = pl.BlockSpec((R, S), lambda i: (i, 0))
    return pl.pallas_call(
        _dft_kernel,
        grid=(S // R,),
        in_specs=[row, row, full, full],
        out_specs=[out, out],
        out_shape=[jax.ShapeDtypeStruct((S, S), BF16)] * 2,
        compiler_params=_params(("parallel",)),
        name="dft_tables",
    )(ca, sa, cb, sb)


def _channel_dft(FW):
    gw = FW // FOURIER_GROUPS
    k = np.arange(gw)
    ang = 2.0 * np.pi * ((k[:, None] * k[None, :]) % gw) / gw
    cs = np.zeros((FW, 2 * FW), np.float64)
    for g in range(FOURIER_GROUPS):
        cs[g * gw:(g + 1) * gw, g * gw:(g + 1) * gw] = np.cos(ang)
        cs[g * gw:(g + 1) * gw, FW + g * gw:FW + (g + 1) * gw] = np.sin(ang)
    return jnp.asarray(cs, BF16)


def _four_kernel(c_ref, s_ref, xcs_ref, wf_ref, o_ref, *, FW, scale):
    y = _dot(c_ref[...], xcs_ref[0, :, :FW]) - _dot(s_ref[...], xcs_ref[0, :, FW:])
    o_ref[0] = _dot((y * scale).astype(BF16), wf_ref[...]).astype(BF16)


def _fourier(xcs, tabs, wf_bd):
    B, S, FW2 = xcs.shape
    FW = FW2 // 2
    tq = min(512, S)
    scale = 1.0 / math.sqrt(S * (FW // FOURIER_GROUPS))
    return pl.pallas_call(
        functools.partial(_four_kernel, FW=FW, scale=scale),
        grid=(S // tq, B),
        in_specs=[pl.BlockSpec((tq, S), lambda i, b: (i, 0)),
                  pl.BlockSpec((tq, S), lambda i, b: (i, 0)),
                  pl.BlockSpec((1, S, FW2), lambda i, b: (b, 0, 0)),
                  pl.BlockSpec((FW, FW), lambda i, b: (0, 0))],
        out_specs=pl.BlockSpec((1, tq, FW), lambda i, b: (b, i, 0)),
        out_shape=jax.ShapeDtypeStruct((B, S, FW), BF16),
        compiler_params=_params(("parallel", "arbitrary")),
        name="fourier_mixer",
    )(tabs[0], tabs[1], xcs, wf_bd)


def _softmax_pv(s, sink, v):
    m = jnp.maximum(jnp.max(s, axis=-1, keepdims=True), sink)
    p = jnp.exp(s - m)
    den = jnp.sum(p, axis=-1, keepdims=True) + jnp.exp(sink - m)
    return _dot(p.astype(BF16), v) / den


def _attn_ctx_kernel(sink_ref, q_ref, k_ref, v_ref, o_ref, *, G):
    q = q_ref[0]
    k = k_ref[0].astype(BF16)
    v = v_ref[0].astype(BF16)
    for kh in range(N_KV_HEADS):
        kk = k[:, kh * HEAD_DIM:(kh + 1) * HEAD_DIM]
        vv = v[:, kh * HEAD_DIM:(kh + 1) * HEAD_DIM]
        for g in range(G):
            hd = kh * G + g
            s = _dot_nt(q[:, hd * HEAD_DIM:(hd + 1) * HEAD_DIM], kk)
            o = _softmax_pv(s, sink_ref[hd], vv)
            o_ref[0, :, hd * HEAD_DIM:(hd + 1) * HEAD_DIM] = o.astype(BF16)


def _attn_ctx(sink, q, k, v):
    B, S, AW = q.shape
    KW = k.shape[2]
    G = AW // HEAD_DIM // N_KV_HEADS
    return pl.pallas_call(
        functools.partial(_attn_ctx_kernel, G=G),
        grid=(B,),
        in_specs=[pl.BlockSpec(memory_space=pltpu.SMEM),
                  pl.BlockSpec((1, S, AW), lambda b: (b, 0, 0)),
                  pl.BlockSpec((1, S, KW), lambda b: (b, 0, 0)),
                  pl.BlockSpec((1, S, KW), lambda b: (b, 0, 0))],
        out_specs=pl.BlockSpec((1, S, AW), lambda b: (b, 0, 0)),
        out_shape=jax.ShapeDtypeStruct((B, S, AW), BF16),
        compiler_params=_params(("parallel",)),
        name="attn_context",
    )(sink, q, k, v)


def _attn_lat_kernel(sink_ref, q_ref, kp_ref, kc_ref, kn_ref, vp_ref, vc_ref, vn_ref, ck_ref, cv_ref, o_ref,
                     *, S, G):
    i = pl.program_id(1)
    blk = ATTN_BLOCK
    q = q_ref[0]
    kcat = jnp.concatenate([kp_ref[0], kc_ref[0], kn_ref[0], ck_ref[0]], axis=0).astype(BF16)
    vcat = jnp.concatenate([vp_ref[0], vc_ref[0], vn_ref[0], cv_ref[0]], axis=0).astype(BF16)
    nk = kcat.shape[0]
    qi = lax.broadcasted_iota(jnp.int32, (blk, nk), 0)
    kj = lax.broadcasted_iota(jnp.int32, (blk, nk), 1)
    pos = i * blk - blk + kj
    local_ok = jnp.where(jnp.abs(kj - blk - qi) <= WINDOW,
                         jnp.where(pos >= 0, jnp.where(pos < S, 1, 0), 0), 0)
    ok = jnp.where(kj >= 3 * blk, 1, local_ok) > 0
    ok = jnp.concatenate([ok] * G, axis=0)
    for kh in range(N_KV_HEADS):
        kk = kcat[:, kh * HEAD_DIM:(kh + 1) * HEAD_DIM]
        vv = vcat[:, kh * HEAD_DIM:(kh + 1) * HEAD_DIM]
        qs = jnp.concatenate([q[:, (kh * G + g) * HEAD_DIM:(kh * G + g + 1) * HEAD_DIM] for g in range(G)], axis=0)
        sink = jnp.concatenate([jnp.full((blk, 1), sink_ref[kh * G + g], F32) for g in range(G)], axis=0)
        s = jnp.where(ok, _dot_nt(qs, kk), NEG_INF)
        o = _softmax_pv(s, sink, vv)
        for g in range(G):
            hd = kh * G + g
            o_ref[0, :, hd * HEAD_DIM:(hd + 1) * HEAD_DIM] = o[g * blk:(g + 1) * blk].astype(BF16)


def _attn_lat(sink, q, k, v, ck, cv):
    B, S, AW = q.shape
    KW = k.shape[2]
    C = ck.shape[1]
    G = AW // HEAD_DIM // N_KV_HEADS
    nb = S // ATTN_BLOCK
    prev = pl.BlockSpec((1, ATTN_BLOCK, KW), lambda b, i: (b, jnp.maximum(i - 1, 0), 0))
    cur = pl.BlockSpec((1, ATTN_BLOCK, KW), lambda b, i: (b, i, 0))
    nxt = pl.BlockSpec((1, ATTN_BLOCK, KW), lambda b, i: (b, jnp.minimum(i + 1, nb - 1), 0))
    ctx = pl.BlockSpec((1, C, KW), lambda b, i: (b, 0, 0))
    return pl.pallas_call(
        functools.partial(_attn_lat_kernel, S=S, G=G),
        grid=(B, nb),
        in_specs=[pl.BlockSpec(memory_space=pltpu.SMEM),
                  pl.BlockSpec((1, ATTN_BLOCK, AW), lambda b, i: (b, i, 0)),
                  prev, cur, nxt, prev, cur, nxt, ctx, ctx],
        out_specs=pl.BlockSpec((1, ATTN_BLOCK, AW), lambda b, i: (b, i, 0)),
        out_shape=jax.ShapeDtypeStruct((B, S, AW), BF16),
        compiler_params=_params(("parallel", "parallel")),
        name="attn_latent",
    )(sink, q, k, k, k, v, v, v, ck, cv)


def _out_proj_kernel(pool_ref, four_ref, attn_ref, x_ref, mod_ref, ln_ref, w_ref, x1_ref, h2t_ref, *, D, PW, FW):
    mixed = (_dot(pool_ref[...], w_ref[0:PW, :]) + _dot(four_ref[...], w_ref[PW:PW + FW, :])
             + _dot(attn_ref[...], w_ref[PW + FW:, :]))
    mod = mod_ref[0]
    g1 = mod[:, 2 * D:3 * D]
    sh2 = mod[:, 3 * D:4 * D]
    sc2 = mod[:, 4 * D:5 * D]
    x1 = x_ref[...] + g1 * mixed
    h2 = _rms(x1, ln_ref[...]) * (1.0 + sc2) + sh2
    x1_ref[...] = x1
    h2t_ref[...] = h2.T.astype(BF16)


def _out_proj(pool_o, four_o, attn_o, x, mod3, ln, w_out, seq_len, per_seq):
    T, D = x.shape
    PW, FW, AW = pool_o.shape[1], four_o.shape[1], attn_o.shape[1]
    tm = min(256, seq_len)
    return pl.pallas_call(
        functools.partial(_out_proj_kernel, D=D, PW=PW, FW=FW),
        grid=(T // tm,),
        in_specs=[pl.BlockSpec((tm, PW), lambda i: (i, 0)),
                  pl.BlockSpec((tm, FW), lambda i: (i, 0)),
                  pl.BlockSpec((tm, AW), lambda i: (i, 0)),
                  pl.BlockSpec((tm, D), lambda i: (i, 0)),
                  _mod_spec(6 * D, tm, seq_len, per_seq),
                  pl.BlockSpec((1, D), lambda i: (0, 0)),
                  pl.BlockSpec((PW + FW + AW, D), lambda i: (0, 0))],
        out_specs=[pl.BlockSpec((tm, D), lambda i: (i, 0)),
                   pl.BlockSpec((D, tm), lambda i: (0, i))],
        out_shape=[jax.ShapeDtypeStruct((T, D), F32), jax.ShapeDtypeStruct((D, T), BF16)],
        compiler_params=_params(("parallel",)),
        name="out_proj",
    )(pool_o, four_o, attn_o, x, mod3, ln, w_out)


def _top_values(vals, k, ranked=False):
    out = []
    rank = jnp.full(vals.shape, float(vals.shape[0]), F32) if ranked else None
    for r in range(k):
        m = jnp.max(vals, axis=0, keepdims=True)
        out.append(m)
        hit = vals == m
        if ranked:
            rank = jnp.where(hit, float(r), rank)
        if r + 1 < k:
            vals = jnp.where(hit, -jnp.inf, vals)
    return (out, rank) if ranked else out


def _peer_scores_kernel(xt_ref, wqt_ref, sk_ref, a1_ref, cnt_ref, a2_ref, r2_ref, qt_ref, cand_ref, *, H, K):
    half = sk_ref.shape[3]
    qt_ref[...] = _dot(wqt_ref[...], xt_ref[...]).astype(BF16)
    pairs = [(i, j) for i in range(K) for j in range(K) if (i + 1) * (j + 1) <= K]
    cand_ref[...] = jnp.full(cand_ref.shape, -jnp.inf, F32)
    for h in range(H):
        s1 = _dot(sk_ref[h, 0], qt_ref[(2 * h) * half:(2 * h + 1) * half, :])
        s2 = _dot(sk_ref[h, 1], qt_ref[(2 * h + 1) * half:(2 * h + 2) * half, :])
        v1 = _top_values(s1, K)
        v2, rank2 = _top_values(s2, K, ranked=True)
        sums = {}
        for n, (i, j) in enumerate(pairs):
            sums[(i, j)] = v1[i] + v2[j]
            cand_ref[n:n + 1, :] = sums[(i, j)]
        best = _top_values(cand_ref[...], K)
        theta, top = best[K - 1], best[0]
        z = jnp.exp(best[0] - top)
        for r in range(1, K):
            z = z + jnp.exp(best[r] - top)
        cnt = jnp.zeros(s1.shape, F32)
        for i in range(K):
            ci = jnp.zeros(theta.shape, F32)
            for j in range(K):
                if (i, j) in sums:
                    ci = ci + jnp.where(sums[(i, j)] >= theta, 1.0, 0.0)
            cnt = jnp.where(s1 == v1[i], ci, cnt)
        a1_ref[:, h, :] = jnp.exp(s1 - v1[0]) * (1.0 / z)
        cnt_ref[:, h, :] = cnt
        a2 = jnp.exp(s2 - v2[0]).astype(BF16)
        rank2 = rank2.astype(BF16)
        for lc in range(s2.shape[1] // 128):
            ls = slice(lc * 128, (lc + 1) * 128)
            a2_ref[lc, h] = pltpu.bitcast(a2[:, ls], jnp.int32)
            r2_ref[lc, h] = pltpu.bitcast(rank2[:, ls], jnp.int32)


def _peer_scores(h2t, wqt, sk):
    D, T = h2t.shape
    H, _, NK, half = sk.shape
    tn = min(256, T)
    n_pairs = sum(1 for i in range(PEER_TOPK) for j in range(PEER_TOPK) if (i + 1) * (j + 1) <= PEER_TOPK)
    rows = pl.BlockSpec((NK, H, tn), lambda i: (0, 0, i))
    packed = pl.BlockSpec((tn // 128, H, NK // 2, 128), lambda i: (i, 0, 0, 0))
    return pl.pallas_call(
        functools.partial(_peer_scores_kernel, H=H, K=PEER_TOPK),
        grid=(T // tn,),
        in_specs=[pl.BlockSpec((D, tn), lambda i: (0, i)),
                  pl.BlockSpec(wqt.shape, lambda i: (0, 0)),
                  pl.BlockSpec(sk.shape, lambda i: (0, 0, 0, 0))],
        out_specs=[rows, rows, packed, packed],
        out_shape=[jax.ShapeDtypeStruct((NK, H, T), F32)] * 2 + [jax.ShapeDtypeStruct((T // 128, H, NK // 2, 128), jnp.int32)] * 2,
        scratch_shapes=[pltpu.VMEM((wqt.shape[0], tn), BF16),
                        pltpu.VMEM((-(-n_pairs // 8) * 8, tn), F32)],
        compiler_params=_params(("parallel",)),
        name="peer_scores",
    )(h2t, wqt, sk)


def _gelu_tanh(x):
    return x * (0.5 * (1.0 + jnp.tanh(math.sqrt(2.0 / math.pi) * (x + 0.044715 * (x * x * x)))))


def _zero_of(v):
    u = pltpu.bitcast(v, jnp.uint32)
    return lax.shift_right_logical(lax.shift_right_logical(u, jnp.uint32(16)), jnp.uint32(16))


def _tied(w, zero):
    if zero is None:
        return w
    k, n = w.shape
    parts = []
    for kb in range(0, k, MXU_DEPTH):
        head = pltpu.bitcast(w[kb:kb + 16, :], jnp.uint32) + jnp.tile(zero, (1, n // 128))
        parts += [pltpu.bitcast(head, BF16), w[kb + 16:kb + MXU_DEPTH, :]]
    return jnp.concatenate(parts, axis=0)


def _peer_dense_kernel(*refs, D, H, NK, R, NJ, final):
    if final:
        (xt_ref, u_ref, vt_ref, a1_ref, cnt_ref, a2_ref, r2_ref, x1_ref, mod_ref, fn_ref, o_ref,
         acc_ref, act0_ref, act1_ref, p0_ref, p1_ref) = refs
    else:
        (xt_ref, u_ref, vt_ref, a1_ref, cnt_ref, a2_ref, r2_ref, x1_ref, mod_ref, o_ref,
         acc_ref, act0_ref, act1_ref, p0_ref, p1_ref) = refs
    s = pl.program_id(0)
    tn = xt_ref.shape[1]

    @pl.when(s == 0)
    def _():
        acc_ref[...] = jnp.zeros(acc_ref.shape, F32)
        act1_ref[...] = jnp.zeros(act1_ref.shape, BF16)
        p0_ref[...] = jnp.zeros(p0_ref.shape, BF16)

    ec = R * NK
    n_lc = tn // 128

    def gate_block(r, lc, act_b, p_b):
        ls = slice(lc * 128, (lc + 1) * 128)
        rows = slice(r * NK, (r + 1) * NK)
        a1_heads = a1_ref[r, :, ls]
        cnt_heads = cnt_ref[r, :, ls]
        gate = None
        for h in range(H):
            a1 = jnp.broadcast_to(a1_heads[h:h + 1, :], (NK, 128)).astype(BF16)
            cnt = jnp.broadcast_to(cnt_heads[h:h + 1, :], (NK, 128)).astype(BF16)
            rank2 = pltpu.bitcast(r2_ref[lc, h], BF16)
            a2 = pltpu.bitcast(a2_ref[lc, h], BF16)
            term = a1 * jnp.where(rank2 < cnt, a2, jnp.zeros((), BF16))
            gate = term if gate is None else gate + term
        p = gate * act_b[lc, rows, :]
        p_b[lc, rows, :] = p
        return _zero_of(p[0:16, :])

    def stages(act_a, act_b, p_b, p_c):
        blocks = [(r, lc) for r in range(R) for lc in range(n_lc)]
        per_group = len(blocks) // (2 * PEER_PIECES)
        groups = [blocks[g * per_group:(g + 1) * per_group] for g in range(2 * PEER_PIECES)]

        def gate_group(g):
            zero = None
            for r, lc in groups[g]:
                z = gate_block(r, lc, act_b, p_b)
                zero = z if zero is None else zero | z
            return zero

        xt = xt_ref[...]
        pc = jnp.concatenate([p_c[lc] for lc in range(n_lc)], axis=1)
        tie = None
        for q in range(PEER_PIECES):
            ra = slice(q * (ec // PEER_PIECES), (q + 1) * (ec // PEER_PIECES))
            hq = _dot(u_ref[ra, :], _tied(xt, tie))
            tie = gate_group(2 * q)
            for r0 in range(ra.start, ra.stop, GELU_ROWS):
                for lc in range(n_lc):
                    ls = slice(lc * 128, (lc + 1) * 128)
                    act_a[lc, r0:r0 + GELU_ROWS, :] = _gelu_tanh(
                        hq[r0 - ra.start:r0 - ra.start + GELU_ROWS, ls].astype(BF16))
            rc = slice(q * (D // PEER_PIECES), (q + 1) * (D // PEER_PIECES))
            acc_ref[rc, :] += _dot(vt_ref[0, rc, :], _tied(pc, tie))
            tie = gate_group(2 * q + 1)

    @pl.when(s % 2 == 0)
    def _():
        stages(act0_ref, act1_ref, p1_ref, p0_ref)

    @pl.when(s % 2 == 1)
    def _():
        stages(act1_ref, act0_ref, p0_ref, p1_ref)

    c = s - 2

    @pl.when(jnp.logical_and(c >= 0, c % NJ == NJ - 1))
    def _():
        g2 = mod_ref[0][:, 5 * D:6 * D]
        x2 = x1_ref[...] + g2 * acc_ref[...].T
        o_ref[...] = _rms(x2, fn_ref[...]) if final else x2
        acc_ref[...] = jnp.zeros(acc_ref.shape, F32)


def _peer_dense(h2t, u, vt, a1, cnt, a2, r2, x1, mod3, final_norm, seq_len, per_seq):
    D, T = h2t.shape
    E = u.shape[0]
    NK, H, _ = a1.shape
    R = PEER_CHUNK_KEYS
    ec = R * NK
    vt = vt.reshape(E // ec, ec, D).transpose(0, 2, 1)
    tn = min(512, seq_len if per_seq else T)
    final = final_norm is not None
    nj = E // ec
    total = (T // tn) * nj
    tiles_per_seq = max(seq_len // tn, 1)

    def item(lag):
        return lambda s: jnp.clip(s - lag, 0, total - 1)

    ia, ib, ic = item(0), item(1), item(2)
    mod_row = (lambda s: 1 + (ic(s) // nj) // tiles_per_seq) if per_seq else (lambda s: 0)
    tab = pl.BlockSpec((tn // 128, H, NK // 2, 128), lambda s: (ib(s) // nj, 0, 0, 0))
    row = pl.BlockSpec((R, H, tn), lambda s: (ib(s) % nj, 0, ib(s) // nj))
    in_specs = [pl.BlockSpec((D, tn), lambda s: (0, ia(s) // nj)),
                pl.BlockSpec((ec, D), lambda s: (ia(s) % nj, 0)),
                pl.BlockSpec((1, D, ec), lambda s: (ic(s) % nj, 0, 0)),
                row, row, tab, tab,
                pl.BlockSpec((tn, D), lambda s: (ic(s) // nj, 0)),
                pl.BlockSpec((1, 1, 6 * D), lambda s: (mod_row(s), 0, 0))]
    args = [h2t, u, vt, a1, cnt, a2, r2, x1, mod3]
    if final:
        in_specs.append(pl.BlockSpec((1, D), lambda s: (0, 0)))
        args.append(final_norm)
    return pl.pallas_call(
        functools.partial(_peer_dense_kernel, D=D, H=H, NK=NK, R=R, NJ=nj, final=final),
        grid=(total + 2,),
        in_specs=in_specs,
        out_specs=pl.BlockSpec((tn, D), lambda s: (ic(s) // nj, 0)),
        out_shape=jax.ShapeDtypeStruct((T, D), F32),
        scratch_shapes=[pltpu.VMEM((D, tn), F32)] + [pltpu.VMEM((tn // 128, ec, 128), BF16)] * 4,
        compiler_params=_params(("arbitrary",)),
        name="peer_dense_final" if final else "peer_dense",
    )(*args)


def _rope_tables(S):
    pos = np.arange(S)
    inv_freq = ROPE_THETA ** (-(np.arange(ROPE_HALF // 2, dtype=np.float64) * 2.0 / ROPE_HALF))
    d = np.arange(HEAD_DIM)
    p = np.where(d[None, :] < ROPE_HALF, (pos // GRID_W)[:, None], (pos % GRID_W)[:, None])
    ang = p * inv_freq[d % (ROPE_HALF // 2)][None, :]
    sign = np.where((d % ROPE_HALF) < ROPE_HALF // 2, -1.0, 1.0)[None, :]
    cos = np.tile(np.cos(ang), (1, 128 // HEAD_DIM))
    sin = np.tile(np.sin(ang) * sign, (1, 128 // HEAD_DIM))
    return jnp.asarray(cos, F32), jnp.asarray(sin, F32)


def _block_diag(w):
    g, a, b = w.shape
    out = jnp.zeros((g * a, g * b), w.dtype)
    for i in range(g):
        out = out.at[i * a:(i + 1) * a, i * b:(i + 1) * b].set(w[i])
    return out


def kernel(x_prompt, x_sample, cache_k, cache_v, c, c_ctx, w_ada, b_ada, norm_mix, norm_ffn, w_in, w_pool,
           pool_scale, w_fourier, attn_sink, w_out, peer_w_query, peer_sub_keys, peer_u, peer_v, final_norm):
    B, S, D = x_prompt.shape
    BD, SD, _ = x_sample.shape
    L = w_ada.shape[0]
    C = cache_k.shape[2]
    PW = w_pool.shape[1] * w_pool.shape[2]
    FW = w_fourier.shape[1] * w_fourier.shape[2]
    KW = cache_k.shape[3] * cache_k.shape[4]
    AW = w_in.shape[2] - PW - FW - 2 * KW
    dims = (PW, FW, AW, KW)
    assert BD + 1 <= MOD_ROWS and SD % ATTN_BLOCK == 0 and SD % GRID_W == 0

    cvec = jnp.concatenate([c_ctx[None, :], c, jnp.zeros((MOD_ROWS - 1 - BD, D), F32)], axis=0)
    mod = _ada(cvec, w_ada, b_ada).reshape(L, MOD_ROWS, 1, 6 * D)
    cs = _channel_dft(FW)
    tabs_p = _dft_tables(S)
    tabs_s = _dft_tables(SD)
    rope = _rope_tables(SD)

    xp = x_prompt.reshape(B * S, D)
    xs = x_sample.reshape(BD * SD, D)
    new_k, new_v = [], []
    for l in range(L):
        last = l == L - 1
        w_in_l = w_in[l].astype(BF16)
        w_pool_l = _block_diag(w_pool[l]).astype(BF16)
        w_four_l = _block_diag(w_fourier[l]).astype(BF16)
        w_out_l = w_out[l].astype(BF16)
        wqt_l = peer_w_query[l].T.astype(BF16)
        sk_l = peer_sub_keys[l].astype(BF16)
        u_l = peer_u[l].astype(BF16)
        pv_l = peer_v[l].astype(BF16)
        ln1 = norm_mix[l][None, :]
        ln2 = norm_ffn[l][None, :]
        scale_l = pool_scale[l][None, :]
        sink_l = attn_sink[l]
        fn = final_norm[None, :] if last else None
        ck = cache_k[:, l].reshape(BD, C, KW)
        cv = cache_v[:, l].reshape(BD, C, KW)

        def layer(x, nb, seq, per_seq, rope_tabs, tabs):
            pool_in, xcs, q, k, v = _in_proj(x, mod[l], ln1, w_in_l, cs, rope_tabs, seq, per_seq, dims)
            pool_o = _pool(pool_in.reshape(nb, seq, PW), w_pool_l, scale_l)
            four_o = _fourier(xcs.reshape(nb, seq, 2 * FW), tabs, w_four_l)
            q3, k3, v3 = q.reshape(nb, seq, AW), k.reshape(nb, seq, KW), v.reshape(nb, seq, KW)
            if per_seq:
                attn_o = _attn_lat(sink_l, q3, k3, v3, ck, cv)
            else:
                attn_o = _attn_ctx(sink_l, q3, k3, v3)
            x1, h2t = _out_proj(pool_o.reshape(-1, PW), four_o.reshape(-1, FW), attn_o.reshape(-1, AW),
                                x, mod[l], ln2, w_out_l, seq, per_seq)
            a1, cnt, a2, r2 = _peer_scores(h2t, wqt_l, sk_l)
            x2 = _peer_dense(h2t, u_l, pv_l, a1, cnt, a2, r2, x1, mod[l], fn, seq, per_seq)
            return x2, k3, v3

        xp, k_l, v_l = layer(xp, B, S, False, None, tabs_p)
        xs, _, _ = layer(xs, BD, SD, True, rope, tabs_s)
        new_k.append(k_l.reshape(B, S, N_KV_HEADS, HEAD_DIM))
        new_v.append(v_l.reshape(B, S, N_KV_HEADS, HEAD_DIM))
    return (xp.reshape(B, S, D), xs.reshape(BD, SD, D), jnp.stack(new_k, axis=1), jnp.stack(new_v, axis=1))
```

```python
import functools
import math

import numpy as np
import jax
import jax.numpy as jnp
from jax import lax
from jax.experimental import pallas as pl
from jax.experimental.pallas import tpu as pltpu

F32 = jnp.float32
BF16 = jnp.bfloat16

HEAD_DIM = 64
N_KV_HEADS = 2
POOL_WINDOWS = (2, 4, 8, 16)
POOL_HALO = max(POOL_WINDOWS) // 2
FOURIER_GROUPS = 4
GRID_W = 64
WINDOW = 128
ATTN_BLOCK = 128
ROPE_THETA = 10000.0
ROPE_HALF = HEAD_DIM // 2
PEER_TOPK = 16
NORM_EPS = 1e-6
NEG_INF = -1e30
DFT_ROWS = 64
MOD_ROWS = 16
V7X_VMEM_LIMIT = 48 * 1024 * 1024
MXU_DEPTH = 256
PEER_TOKEN_TILE = 512
PEER_CHUNK_KEYS = 8
TIE_MARKS = 8
PEER_PIECES = 8
GELU_ROWS = 32


def _params(semantics):
    return pltpu.CompilerParams(dimension_semantics=semantics, vmem_limit_bytes=V7X_VMEM_LIMIT)


def _dot(a, b):
    return jnp.dot(a, b, preferred_element_type=F32)


def _dot_nt(a, b):
    return lax.dot_general(a, b, (((1,), (1,)), ((), ())), preferred_element_type=F32)


def _rms(x, g):
    return x * lax.rsqrt(jnp.mean(x * x, axis=-1, keepdims=True) + NORM_EPS) * g


def _ada_kernel(c_ref, w_ref, b_ref, o_ref):
    cv = c_ref[...]
    act = cv * (1.0 / (1.0 + jnp.exp(-cv)))
    o_ref[0] = _dot(act.astype(BF16), w_ref[0].astype(BF16)) + b_ref[0]


def _ada(cvec, w_ada, b_ada):
    L, D, N = w_ada.shape
    tn = N // 4
    return pl.pallas_call(
        _ada_kernel,
        grid=(L, N // tn),
        in_specs=[pl.BlockSpec((MOD_ROWS, D), lambda l, j: (0, 0)),
                  pl.BlockSpec((1, D, tn), lambda l, j: (l, 0, j)),
                  pl.BlockSpec((1, 1, tn), lambda l, j: (l, 0, j))],
        out_specs=pl.BlockSpec((1, MOD_ROWS, tn), lambda l, j: (l, 0, j)),
        out_shape=jax.ShapeDtypeStruct((L, MOD_ROWS, N), F32),
        compiler_params=_params(("parallel", "parallel")),
        name="ada_mod",
    )(cvec, w_ada, b_ada.reshape(L, 1, N))


def _rope_apply(x, cos, sin):
    lane = lax.broadcasted_iota(jnp.int32, (x.shape[0], 128), 1)
    first = (lane % ROPE_HALF) < (ROPE_HALF // 2)
    outs = []
    for g in range(x.shape[1] // 128):
        xg = x[:, g * 128:(g + 1) * 128]
        partner = jnp.where(first, pltpu.roll(xg, 128 - ROPE_HALF // 2, axis=1),
                            pltpu.roll(xg, ROPE_HALF // 2, axis=1))
        outs.append(xg * cos + partner * sin)
    return outs[0] if len(outs) == 1 else jnp.concatenate(outs, axis=1)


def _in_proj_kernel(*refs, D, PW, FW, AW, KW, rope):
    if rope:
        x_ref, mod_ref, ln_ref, w_ref, cs_ref, cos_ref, sin_ref, pool_ref, xcs_ref, q_ref, k_ref, v_ref = refs
    else:
        x_ref, mod_ref, ln_ref, w_ref, cs_ref, pool_ref, xcs_ref, q_ref, k_ref, v_ref = refs
    mod = mod_ref[0]
    sh1 = mod[:, 0:D]
    sc1 = mod[:, D:2 * D]
    h = _rms(x_ref[...], ln_ref[...]) * (1.0 + sc1) + sh1
    proj = _dot(h.astype(BF16), w_ref[...])
    o1, o2 = PW, PW + FW
    o3 = o2 + AW
    o4 = o3 + KW
    pool_ref[...] = proj[:, :o1]
    xcs_ref[...] = _dot(proj[:, o1:o2].astype(BF16), cs_ref[...]).astype(BF16)
    q = proj[:, o2:o3]
    k = proj[:, o3:o4]
    if rope:
        q = _rope_apply(q, cos_ref[...], sin_ref[...])
        k = _rope_apply(k, cos_ref[...], sin_ref[...])
    q_ref[...] = (q * (HEAD_DIM ** -0.5)).astype(BF16)
    k_ref[...] = k
    v_ref[...] = proj[:, o4:]


def _mod_spec(n_cols, tile, seq_len, per_seq):
    tiles_per_seq = max(seq_len // tile, 1)
    if per_seq:
        return pl.BlockSpec((1, 1, n_cols), lambda i, *_: (1 + i // tiles_per_seq, 0, 0))
    return pl.BlockSpec((1, 1, n_cols), lambda i, *_: (0, 0, 0))


def _in_proj(x, mod3, ln, w_in, cs, rope_tabs, seq_len, per_seq, dims):
    T, D = x.shape
    PW, FW, AW, KW = dims
    tm = min(256, seq_len)
    rope = rope_tabs is not None
    n_in = w_in.shape[1]
    tiles_per_seq = seq_len // tm
    in_specs = [pl.BlockSpec((tm, D), lambda i: (i, 0)),
                _mod_spec(6 * D, tm, seq_len, per_seq),
                pl.BlockSpec((1, D), lambda i: (0, 0)),
                pl.BlockSpec((D, n_in), lambda i: (0, 0)),
                pl.BlockSpec((FW, 2 * FW), lambda i: (0, 0))]
    args = [x, mod3, ln, w_in, cs]
    if rope:
        in_specs += [pl.BlockSpec((tm, 128), lambda i: (i % tiles_per_seq, 0))] * 2
        args += list(rope_tabs)
    outs = [(PW, F32), (2 * FW, BF16), (AW, BF16), (KW, F32), (KW, F32)]
    return pl.pallas_call(
        functools.partial(_in_proj_kernel, D=D, PW=PW, FW=FW, AW=AW, KW=KW, rope=rope),
        grid=(T // tm,),
        in_specs=in_specs,
        out_specs=[pl.BlockSpec((tm, w), lambda i: (i, 0)) for w, _ in outs],
        out_shape=[jax.ShapeDtypeStruct((T, w), dt) for w, dt in outs],
        compiler_params=_params(("parallel",)),
        name="in_proj_rope" if rope else "in_proj",
    )(*args)


def _pool_kernel(xp_ref, w_ref, sc_ref, o_ref, *, S, TP):
    c = pl.program_id(1)
    n = TP + 2 * POOL_HALO
    xs = xp_ref[0, pl.ds(pl.multiple_of(c * TP, 8), n), :]
    width = xs.shape[1]
    run = xs
    centred = []
    for w in POOL_WINDOWS:
        run = run + pltpu.roll(run, n - w // 2, axis=0)
        centred.append(pltpu.roll(run, w // 2, axis=0)[POOL_HALO:POOL_HALO + TP])
    lane = lax.broadcasted_iota(jnp.int32, (TP, width), 1)
    grp = lane // (width // len(POOL_WINDOWS))
    win = centred[-1]
    half = jnp.full((TP, width), POOL_WINDOWS[-1] // 2, jnp.int32)
    for gi in range(len(POOL_WINDOWS) - 2, -1, -1):
        win = jnp.where(grp == gi, centred[gi], win)
        half = jnp.where(grp == gi, POOL_WINDOWS[gi] // 2, half)
    t = c * TP + lax.broadcasted_iota(jnp.int32, (TP, width), 0)
    cnt = (jnp.clip(t + half, 0, S) - jnp.clip(t - half, 0, S)).astype(F32)
    y = win / cnt - xs[POOL_HALO:POOL_HALO + TP]
    o_ref[0] = (_dot(y.astype(BF16), w_ref[...]) * sc_ref[...]).astype(BF16)


def _pool(p_pool, w_bd, scale):
    B, S, W = p_pool.shape
    tp = min(256, S)
    xp = jnp.pad(p_pool, ((0, 0), (POOL_HALO, POOL_HALO), (0, 0)))
    return pl.pallas_call(
        functools.partial(_pool_kernel, S=S, TP=tp),
        grid=(B, S // tp),
        in_specs=[pl.BlockSpec((1, S + 2 * POOL_HALO, W), lambda b, c: (b, 0, 0)),
                  pl.BlockSpec((W, W), lambda b, c: (0, 0)),
                  pl.BlockSpec((1, W), lambda b, c: (0, 0))],
        out_specs=pl.BlockSpec((1, tp, W), lambda b, c: (b, c, 0)),
        out_shape=jax.ShapeDtypeStruct((B, S, W), BF16),
        compiler_params=_params(("parallel", "arbitrary")),
        name="pool_mixer",
    )(xp, w_bd, scale)


def _dft_kernel(ca_ref, sa_ref, cb_ref, sb_ref, c_ref, s_ref):
    ca, sa = ca_ref[0], sa_ref[0]
    cb, sb = cb_ref[...], sb_ref[...]
    c_ref[...] = (ca * cb - sa * sb).astype(BF16)
    s_ref[...] = (sa * cb + ca * sb).astype(BF16)


def _dft_tables(S):
    R = min(DFT_ROWS, S)
    k = np.arange(S, dtype=np.int64)
    a = np.arange(S // R, dtype=np.int64)[:, None]
    b = np.arange(R, dtype=np.int64)[:, None]
    ang_a = 2.0 * np.pi * ((a * R * k) % S) / S
    ang_b = 2.0 * np.pi * ((b * k) % S) / S
    ca = jnp.asarray(np.cos(ang_a), F32).reshape(S // R, 1, S)
    sa = jnp.asarray(np.sin(ang_a), F32).reshape(S // R, 1, S)
    cb = jnp.asarray(np.cos(ang_b), F32)
    sb = jnp.asarray(np.sin(ang_b), F32)
    row = pl.BlockSpec((1, 1, S), lambda i: (i, 0, 0))
    full = pl.BlockSpec((R, S), lambda i: (0, 0))
    out = pl.BlockSpec((R, S), lambda i: (i, 0))
    return pl.pallas_call(
        _dft_kernel,
        grid=(S // R,),
        in_specs=[row, row, full, full],
        out_specs=[out, out],
        out_shape=[jax.ShapeDtypeStruct((S, S), BF16)] * 2,
        compiler_params=_params(("parallel",)),
        name="dft_tables",
    )(ca, sa, cb, sb)


def _channel_dft(FW):
    gw = FW // FOURIER_GROUPS
    k = np.arange(gw)
    ang = 2.0 * np.pi * ((k[:, None] * k[None, :]) % gw) / gw
    cs = np.zeros((FW, 2 * FW), np.float64)
    for g in range(FOURIER_GROUPS):
        cs[g * gw:(g + 1) * gw, g * gw:(g + 1) * gw] = np.cos(ang)
        cs[g * gw:(g + 1) * gw, FW + g * gw:FW + (g + 1) * gw] = np.sin(ang)
    return jnp.asarray(cs, BF16)


def _four_kernel(c_ref, s_ref, xcs_ref, wf_ref, o_ref, *, FW, scale):
    y = _dot(c_ref[...], xcs_ref[0, :, :FW]) - _dot(s_ref[...], xcs_ref[0, :, FW:])
    o_ref[0] = _dot((y * scale).astype(BF16), wf_ref[...]).astype(BF16)


def _fourier(xcs, tabs, wf_bd):
    B, S, FW2 = xcs.shape
    FW = FW2 // 2
    tq = min(512, S)
    scale = 1.0 / math.sqrt(S * (FW // FOURIER_GROUPS))
    return pl.pallas_call(
        functools.partial(_four_kernel, FW=FW, scale=scale),
        grid=(S // tq, B),
        in_specs=[pl.BlockSpec((tq, S), lambda i, b: (i, 0)),
                  pl.BlockSpec((tq, S), lambda i, b: (i, 0)),
                  pl.BlockSpec((1, S, FW2), lambda i, b: (b, 0, 0)),
                  pl.BlockSpec((FW, FW), lambda i, b: (0, 0))],
        out_specs=pl.BlockSpec((1, tq, FW), lambda i, b: (b, i, 0)),
        out_shape=jax.ShapeDtypeStruct((B, S, FW), BF16),
        compiler_params=_params(("parallel", "arbitrary")),
        name="fourier_mixer",
    )(tabs[0], tabs[1], xcs, wf_bd)


def _softmax_pv(s, sink, v):
    m = jnp.maximum(jnp.max(s, axis=-1, keepdims=True), sink)
    p = jnp.exp(s - m)
    den = jnp.sum(p, axis=-1, keepdims=True) + jnp.exp(sink - m)
    return _dot(p.astype(BF16), v) / den


def _attn_ctx_kernel(sink_ref, q_ref, k_ref, v_ref, o_ref, *, G):
    q = q_ref[0]
    k = k_ref[0].astype(BF16)
    v = v_ref[0].astype(BF16)
    for kh in range(N_KV_HEADS):
        kk = k[:, kh * HEAD_DIM:(kh + 1) * HEAD_DIM]
        vv = v[:, kh * HEAD_DIM:(kh + 1) * HEAD_DIM]
        for g in range(G):
            hd = kh * G + g
            s = _dot_nt(q[:, hd * HEAD_DIM:(hd + 1) * HEAD_DIM], kk)
            o = _softmax_pv(s, sink_ref[hd], vv)
            o_ref[0, :, hd * HEAD_DIM:(hd + 1) * HEAD_DIM] = o.astype(BF16)


def _attn_ctx(sink, q, k, v):
    B, S, AW = q.shape
    KW = k.shape[2]
    G = AW // HEAD_DIM // N_KV_HEADS
    return pl.pallas_call(
        functools.partial(_attn_ctx_kernel, G=G),
        grid=(B,),
        in_specs=[pl.BlockSpec(memory_space=pltpu.SMEM),
                  pl.BlockSpec((1, S, AW), lambda b: (b, 0, 0)),
                  pl.BlockSpec((1, S, KW), lambda b: (b, 0, 0)),
                  pl.BlockSpec((1, S, KW), lambda b: (b, 0, 0))],
        out_specs=pl.BlockSpec((1, S, AW), lambda b: (b, 0, 0)),
        out_shape=jax.ShapeDtypeStruct((B, S, AW), BF16),
        compiler_params=_params(("parallel",)),
        name="attn_context",
    )(sink, q, k, v)


def _attn_lat_kernel(sink_ref, q_ref, kp_ref, kc_ref, kn_ref, vp_ref, vc_ref, vn_ref, ck_ref, cv_ref, o_ref,
                     *, S, G):
    i = pl.program_id(1)
    blk = ATTN_BLOCK
    q = q_ref[0]
    kcat = jnp.concatenate([kp_ref[0], kc_ref[0], kn_ref[0], ck_ref[0]], axis=0).astype(BF16)
    vcat = jnp.concatenate([vp_ref[0], vc_ref[0], vn_ref[0], cv_ref[0]], axis=0).astype(BF16)
    nk = kcat.shape[0]
    qi = lax.broadcasted_iota(jnp.int32, (blk, nk), 0)
    kj = lax.broadcasted_iota(jnp.int32, (blk, nk), 1)
    pos = i * blk - blk + kj
    local_ok = jnp.where(jnp.abs(kj - blk - qi) <= WINDOW,
                         jnp.where(pos >= 0, jnp.where(pos < S, 1, 0), 0), 0)
    ok = jnp.where(kj >= 3 * blk, 1, local_ok) > 0
    ok = jnp.concatenate([ok] * G, axis=0)
    for kh in range(N_KV_HEADS):
        kk = kcat[:, kh * HEAD_DIM:(kh + 1) * HEAD_DIM]
        vv = vcat[:, kh * HEAD_DIM:(kh + 1) * HEAD_DIM]
        qs = jnp.concatenate([q[:, (kh * G + g) * HEAD_DIM:(kh * G + g + 1) * HEAD_DIM] for g in range(G)], axis=0)
        sink = jnp.concatenate([jnp.full((blk, 1), sink_ref[kh * G + g], F32) for g in range(G)], axis=0)
        s = jnp.where(ok, _dot_nt(qs, kk), NEG_INF)
        o = _softmax_pv(s, sink, vv)
        for g in range(G):
            hd = kh * G + g
            o_ref[0, :, hd * HEAD_DIM:(hd + 1) * HEAD_DIM] = o[g * blk:(g + 1) * blk].astype(BF16)


def _attn_lat(sink, q, k, v, ck, cv):
    B, S, AW = q.shape
    KW = k.shape[2]
    C = ck.shape[1]
    G = AW // HEAD_DIM // N_KV_HEADS
    nb = S // ATTN_BLOCK
    prev = pl.BlockSpec((1, ATTN_BLOCK, KW), lambda b, i: (b, jnp.maximum(i - 1, 0), 0))
    cur = pl.BlockSpec((1, ATTN_BLOCK, KW), lambda b, i: (b, i, 0))
    nxt = pl.BlockSpec((1, ATTN_BLOCK, KW), lambda b, i: (b, jnp.minimum(i + 1, nb - 1), 0))
    ctx = pl.BlockSpec((1, C, KW), lambda b, i: (b, 0, 0))
    return pl.pallas_call(
        functools.partial(_attn_lat_kernel, S=S, G=G),
        grid=(B, nb),
        in_specs=[pl.BlockSpec(memory_space=pltpu.SMEM),
                  pl.BlockSpec((1, ATTN_BLOCK, AW), lambda b, i: (b, i, 0)),
                  prev, cur, nxt, prev, cur, nxt, ctx, ctx],
        out_specs=pl.BlockSpec((1, ATTN_BLOCK, AW), lambda b, i: (b, i, 0)),
        out_shape=jax.ShapeDtypeStruct((B, S, AW), BF16),
        compiler_params=_params(("parallel", "parallel")),
        name="attn_latent",
    )(sink, q, k, k, k, v, v, v, ck, cv)


def _out_proj_kernel(pool_ref, four_ref, attn_ref, x_ref, mod_ref, ln_ref, w_ref, x1_ref, h2t_ref, *, D, PW, FW):
    mixed = (_dot(pool_ref[...], w_ref[0:PW, :]) + _dot(four_ref[...], w_ref[PW:PW + FW, :])
             + _dot(attn_ref[...], w_ref[PW + FW:, :]))
    mod = mod_ref[0]
    g1 = mod[:, 2 * D:3 * D]
    sh2 = mod[:, 3 * D:4 * D]
    sc2 = mod[:, 4 * D:5 * D]
    x1 = x_ref[...] + g1 * mixed
    h2 = _rms(x1, ln_ref[...]) * (1.0 + sc2) + sh2
    x1_ref[...] = x1
    h2t_ref[...] = h2.T.astype(BF16)


def _out_proj(pool_o, four_o, attn_o, x, mod3, ln, w_out, seq_len, per_seq):
    T, D = x.shape
    PW, FW, AW = pool_o.shape[1], four_o.shape[1], attn_o.shape[1]
    tm = min(256, seq_len)
    return pl.pallas_call(
        functools.partial(_out_proj_kernel, D=D, PW=PW, FW=FW),
        grid=(T // tm,),
        in_specs=[pl.BlockSpec((tm, PW), lambda i: (i, 0)),
                  pl.BlockSpec((tm, FW), lambda i: (i, 0)),
                  pl.BlockSpec((tm, AW), lambda i: (i, 0)),
                  pl.BlockSpec((tm, D), lambda i: (i, 0)),
                  _mod_spec(6 * D, tm, seq_len, per_seq),
                  pl.BlockSpec((1, D), lambda i: (0, 0)),
                  pl.BlockSpec((PW + FW + AW, D), lambda i: (0, 0))],
        out_specs=[pl.BlockSpec((tm, D), lambda i: (i, 0)),
                   pl.BlockSpec((D, tm), lambda i: (0, i))],
        out_shape=[jax.ShapeDtypeStruct((T, D), F32), jax.ShapeDtypeStruct((D, T), BF16)],
        compiler_params=_params(("parallel",)),
        name="out_proj",
    )(pool_o, four_o, attn_o, x, mod3, ln, w_out)


def _top_values(vals, k, ranked=False):
    out = []
    rank = jnp.full(vals.shape, float(vals.shape[0]), F32) if ranked else None
    for r in range(k):
        m = jnp.max(vals, axis=0, keepdims=True)
        out.append(m)
        hit = vals == m
        if ranked:
            rank = jnp.where(hit, float(r), rank)
        if r + 1 < k:
            vals = jnp.where(hit, -jnp.inf, vals)
    return (out, rank) if ranked else out


def _peer_scores_kernel(xt_ref, wqt_ref, sk_ref, a1_ref, cnt_ref, a2_ref, r2_ref, qt_ref, cand_ref, *, H, K):
    half = sk_ref.shape[3]
    qt_ref[...] = _dot(wqt_ref[...], xt_ref[...]).astype(BF16)
    pairs = [(i, j) for i in range(K) for j in range(K) if (i + 1) * (j + 1) <= K]
    cand_ref[...] = jnp.full(cand_ref.shape, -jnp.inf, F32)
    for h in range(H):
        s1 = _dot(sk_ref[h, 0], qt_ref[(2 * h) * half:(2 * h + 1) * half, :])
        s2 = _dot(sk_ref[h, 1], qt_ref[(2 * h + 1) * half:(2 * h + 2) * half, :])
        v1 = _top_values(s1, K)
        v2, rank2 = _top_values(s2, K, ranked=True)
        sums = {}
        for n, (i, j) in enumerate(pairs):
            sums[(i, j)] = v1[i] + v2[j]
            cand_ref[n:n + 1, :] = sums[(i, j)]
        best = _top_values(cand_ref[...], K)
        theta, top = best[K - 1], best[0]
        z = jnp.exp(best[0] - top)
        for r in range(1, K):
            z = z + jnp.exp(best[r] - top)
        cnt = jnp.zeros(s1.shape, F32)
        for i in range(K):
            ci = jnp.zeros(theta.shape, F32)
            for j in range(K):
                if (i, j) in sums:
                    ci = ci + jnp.where(sums[(i, j)] >= theta, 1.0, 0.0)
            cnt = jnp.where(s1 == v1[i], ci, cnt)
        a1_ref[:, h, :] = jnp.exp(s1 - v1[0]) * (1.0 / z)
        cnt_ref[:, h, :] = cnt
        a2 = jnp.exp(s2 - v2[0]).astype(BF16)
        rank2 = rank2.astype(BF16)
        for lc in range(s2.shape[1] // 128):
            ls = slice(lc * 128, (lc + 1) * 128)
            a2_ref[lc, h] = pltpu.bitcast(a2[:, ls], jnp.int32)
            r2_ref[lc, h] = pltpu.bitcast(rank2[:, ls], jnp.int32)


def _peer_scores(h2t, wqt, sk):
    D, T = h2t.shape
    H, _, NK, half = sk.shape
    tn = min(256, T)
    n_pairs = sum(1 for i in range(PEER_TOPK) for j in range(PEER_TOPK) if (i + 1) * (j + 1) <= PEER_TOPK)
    rows = pl.BlockSpec((NK, H, tn), lambda i: (0, 0, i))
    packed = pl.BlockSpec((tn // 128, H, NK // 2, 128), lambda i: (i, 0, 0, 0))
    return pl.pallas_call(
        functools.partial(_peer_scores_kernel, H=H, K=PEER_TOPK),
        grid=(T // tn,),
        in_specs=[pl.BlockSpec((D, tn), lambda i: (0, i)),
                  pl.BlockSpec(wqt.shape, lambda i: (0, 0)),
                  pl.BlockSpec(sk.shape, lambda i: (0, 0, 0, 0))],
        out_specs=[rows, rows, packed, packed],
        out_shape=[jax.ShapeDtypeStruct((NK, H, T), F32)] * 2 + [jax.ShapeDtypeStruct((T // 128, H, NK // 2, 128), jnp.int32)] * 2,
        scratch_shapes=[pltpu.VMEM((wqt.shape[0], tn), BF16),
                        pltpu.VMEM((-(-n_pairs // 8) * 8, tn), F32)],
        compiler_params=_params(("parallel",)),
        name="peer_scores",
    )(h2t, wqt, sk)


def _gelu_tanh(x):
    return x * (0.5 * (1.0 + jnp.tanh(math.sqrt(2.0 / math.pi) * (x + 0.044715 * (x * x * x)))))


def _zero_of(v):
    u = pltpu.bitcast(v, jnp.uint32)
    return lax.shift_right_logical(lax.shift_right_logical(u, jnp.uint32(16)), jnp.uint32(16))


def _tied_rows(x, zeros):
    groups = x.shape[0] // 16
    parts = []
    for g in range(groups):
        blk = x[g * 16:(g + 1) * 16, :]
        zero = zeros[g * len(zeros) // groups]
        if zero is not None:
            head = pltpu.bitcast(blk[:, :MXU_DEPTH], jnp.uint32) + jnp.tile(zero, (1, MXU_DEPTH // 128))
            blk = jnp.concatenate([pltpu.bitcast(head, BF16), blk[:, MXU_DEPTH:]], axis=1)
        parts.append(blk)
    return jnp.concatenate(parts, axis=0)


def _peer_dense_kernel(*refs, D, H, NK, R, NJ, final):
    if final:
        (xt_ref, u_ref, vt_ref, a1_ref, cnt_ref, a2_ref, r2_ref, x1_ref, mod_ref, fn_ref, o_ref,
         acc_ref, act0_ref, act1_ref, p0_ref, p1_ref) = refs
    else:
        (xt_ref, u_ref, vt_ref, a1_ref, cnt_ref, a2_ref, r2_ref, x1_ref, mod_ref, o_ref,
         acc_ref, act0_ref, act1_ref, p0_ref, p1_ref) = refs
    s = pl.program_id(0)
    tn = xt_ref.shape[1]

    @pl.when(s == 0)
    def _():
        acc_ref[...] = jnp.zeros(acc_ref.shape, F32)
        act1_ref[...] = jnp.zeros(act1_ref.shape, BF16)
        p0_ref[...] = jnp.zeros(p0_ref.shape, BF16)

    ec = R * NK
    n_lc = tn // 128

    def gate_block(r, lc, act_b, p_b):
        ls = slice(lc * 128, (lc + 1) * 128)
        rows = slice(r * NK, (r + 1) * NK)
        a1_heads = a1_ref[r, :, ls]
        cnt_heads = cnt_ref[r, :, ls]
        gate = None
        marks = []
        for h in range(H):
            a1 = jnp.broadcast_to(a1_heads[h:h + 1, :], (NK, 128)).astype(BF16)
            cnt = jnp.broadcast_to(cnt_heads[h:h + 1, :], (NK, 128)).astype(BF16)
            rank2 = pltpu.bitcast(r2_ref[lc, h], BF16)
            a2 = pltpu.bitcast(a2_ref[lc, h], BF16)
            term = a1 * jnp.where(rank2 < cnt, a2, jnp.zeros((), BF16))
            gate = term if gate is None else gate + term
            if (h + 1) % (H // TIE_MARKS) == 0 and h + 1 < H:
                marks.append(_zero_of(gate[0:16, :]))
        p = gate * act_b[lc, rows, :]
        p_b[lc, rows, :] = p
        return marks + [_zero_of(p[0:16, :])]

    def stages(act_a, act_b, p_b, p_c):
        blocks = [(r, lc) for r in range(R) for lc in range(n_lc)]
        per_group = len(blocks) // (4 * PEER_PIECES)
        groups = [blocks[g * per_group:(g + 1) * per_group] for g in range(4 * PEER_PIECES)]

        def gate_group(g):
            marks = []
            for r, lc in groups[g]:
                marks += gate_block(r, lc, act_b, p_b)
            return marks

        xt = xt_ref[...]
        pc = jnp.concatenate([p_c[lc] for lc in range(n_lc)], axis=1)
        for q in range(PEER_PIECES):
            ra = slice(q * (ec // PEER_PIECES), (q + 1) * (ec // PEER_PIECES))
            marks = gate_group(4 * q) + gate_group(4 * q + 1)
            hq = _dot(_tied_rows(u_ref[ra, :], marks), xt)
            for r0 in range(ra.start, ra.stop, GELU_ROWS):
                for lc in range(n_lc):
                    ls = slice(lc * 128, (lc + 1) * 128)
                    act_a[lc, r0:r0 + GELU_ROWS, :] = _gelu_tanh(
                        hq[r0 - ra.start:r0 - ra.start + GELU_ROWS, ls].astype(BF16))
            rc = slice(q * (D // PEER_PIECES), (q + 1) * (D // PEER_PIECES))
            marks = gate_group(4 * q + 2) + gate_group(4 * q + 3)
            acc_ref[rc, :] += _dot(_tied_rows(vt_ref[0, rc, :], marks), pc)

    @pl.when(s % 2 == 0)
    def _():
        stages(act0_ref, act1_ref, p1_ref, p0_ref)

    @pl.when(s % 2 == 1)
    def _():
        stages(act1_ref, act0_ref, p0_ref, p1_ref)

    c = s - 2

    @pl.when(jnp.logical_and(c >= 0, c % NJ == NJ - 1))
    def _():
        g2 = mod_ref[0][:, 5 * D:6 * D]
        x2 = x1_ref[...] + g2 * acc_ref[...].T
        o_ref[...] = _rms(x2, fn_ref[...]) if final else x2
        acc_ref[...] = jnp.zeros(acc_ref.shape, F32)


def _peer_dense(h2t, u, vt, a1, cnt, a2, r2, x1, mod3, final_norm, seq_len, per_seq):
    D, T = h2t.shape
    E = u.shape[0]
    NK, H, _ = a1.shape
    R = PEER_CHUNK_KEYS
    ec = R * NK
    vt = vt.reshape(E // ec, ec, D).transpose(0, 2, 1)
    tn = min(PEER_TOKEN_TILE, seq_len if per_seq else T)
    final = final_norm is not None
    nj = E // ec
    total = (T // tn) * nj
    tiles_per_seq = max(seq_len // tn, 1)

    def item(lag):
        return lambda s: jnp.clip(s - lag, 0, total - 1)

    ia, ib, ic = item(0), item(1), item(2)
    mod_row = (lambda s: 1 + (ic(s) // nj) // tiles_per_seq) if per_seq else (lambda s: 0)
    tab = pl.BlockSpec((tn // 128, H, NK // 2, 128), lambda s: (ib(s) // nj, 0, 0, 0))
    row = pl.BlockSpec((R, H, tn), lambda s: (ib(s) % nj, 0, ib(s) // nj))
    in_specs = [pl.BlockSpec((D, tn), lambda s: (0, ia(s) // nj)),
                pl.BlockSpec((ec, D), lambda s: (ia(s) % nj, 0)),
                pl.BlockSpec((1, D, ec), lambda s: (ic(s) % nj, 0, 0)),
                row, row, tab, tab,
                pl.BlockSpec((tn, D), lambda s: (ic(s) // nj, 0)),
                pl.BlockSpec((1, 1, 6 * D), lambda s: (mod_row(s), 0, 0))]
    args = [h2t, u, vt, a1, cnt, a2, r2, x1, mod3]
    if final:
        in_specs.append(pl.BlockSpec((1, D), lambda s: (0, 0)))
        args.append(final_norm)
    return pl.pallas_call(
        functools.partial(_peer_dense_kernel, D=D, H=H, NK=NK, R=R, NJ=nj, final=final),
        grid=(total + 2,),
        in_specs=in_specs,
        out_specs=pl.BlockSpec((tn, D), lambda s: (ic(s) // nj, 0)),
        out_shape=jax.ShapeDtypeStruct((T, D), F32),
        scratch_shapes=[pltpu.VMEM((D, tn), F32)] + [pltpu.VMEM((tn // 128, ec, 128), BF16)] * 4,
        compiler_params=_params(("arbitrary",)),
        name="peer_dense_final" if final else "peer_dense",
    )(*args)


def _rope_tables(S):
    pos = np.arange(S)
    inv_freq = ROPE_THETA ** (-(np.arange(ROPE_HALF // 2, dtype=np.float64) * 2.0 / ROPE_HALF))
    d = np.arange(HEAD_DIM)
    p = np.where(d[None, :] < ROPE_HALF, (pos // GRID_W)[:, None], (pos % GRID_W)[:, None])
    ang = p * inv_freq[d % (ROPE_HALF // 2)][None, :]
    sign = np.where((d % ROPE_HALF) < ROPE_HALF // 2, -1.0, 1.0)[None, :]
    cos = np.tile(np.cos(ang), (1, 128 // HEAD_DIM))
    sin = np.tile(np.sin(ang) * sign, (1, 128 // HEAD_DIM))
    return jnp.asarray(cos, F32), jnp.asarray(sin, F32)


def _block_diag(w):
    g, a, b = w.shape
    out = jnp.zeros((g * a, g * b), w.dtype)
    for i in range(g):
        out = out.at[i * a:(i + 1) * a, i * b:(i + 1) * b].set(w[i])
    return out


def kernel(x_prompt, x_sample, cache_k, cache_v, c, c_ctx, w_ada, b_ada, norm_mix, norm_ffn, w_in, w_pool,
           pool_scale, w_fourier, attn_sink, w_out, peer_w_query, peer_sub_keys, peer_u, peer_v, final_norm):
    B, S, D = x_prompt.shape
    BD, SD, _ = x_sample.shape
    L = w_ada.shape[0]
    C = cache_k.shape[2]
    PW = w_pool.shape[1] * w_pool.shape[2]
    FW = w_fourier.shape[1] * w_fourier.shape[2]
    KW = cache_k.shape[3] * cache_k.shape[4]
    AW = w_in.shape[2] - PW - FW - 2 * KW
    dims = (PW, FW, AW, KW)
    assert BD + 1 <= MOD_ROWS and SD % ATTN_BLOCK == 0 and SD % GRID_W == 0

    cvec = jnp.concatenate([c_ctx[None, :], c, jnp.zeros((MOD_ROWS - 1 - BD, D), F32)], axis=0)
    mod = _ada(cvec, w_ada, b_ada).reshape(L, MOD_ROWS, 1, 6 * D)
    cs = _channel_dft(FW)
    tabs_p = _dft_tables(S)
    tabs_s = _dft_tables(SD)
    rope = _rope_tables(SD)

    xp = x_prompt.reshape(B * S, D)
    xs = x_sample.reshape(BD * SD, D)
    new_k, new_v = [], []
    for l in range(L):
        last = l == L - 1
        w_in_l = w_in[l].astype(BF16)
        w_pool_l = _block_diag(w_pool[l]).astype(BF16)
        w_four_l = _block_diag(w_fourier[l]).astype(BF16)
        w_out_l = w_out[l].astype(BF16)
        wqt_l = peer_w_query[l].T.astype(BF16)
        sk_l = peer_sub_keys[l].astype(BF16)
        u_l = peer_u[l].astype(BF16)
        pv_l = peer_v[l].astype(BF16)
        ln1 = norm_mix[l][None, :]
        ln2 = norm_ffn[l][None, :]
        scale_l = pool_scale[l][None, :]
        sink_l = attn_sink[l]
        fn = final_norm[None, :] if last else None
        ck = cache_k[:, l].reshape(BD, C, KW)
        cv = cache_v[:, l].reshape(BD, C, KW)

        def layer(x, nb, seq, per_seq, rope_tabs, tabs):
            pool_in, xcs, q, k, v = _in_proj(x, mod[l], ln1, w_in_l, cs, rope_tabs, seq, per_seq, dims)
            pool_o = _pool(pool_in.reshape(nb, seq, PW), w_pool_l, scale_l)
            four_o = _fourier(xcs.reshape(nb, seq, 2 * FW), tabs, w_four_l)
            q3, k3, v3 = q.reshape(nb, seq, AW), k.reshape(nb, seq, KW), v.reshape(nb, seq, KW)
            if per_seq:
                attn_o = _attn_lat(sink_l, q3, k3, v3, ck, cv)
            else:
                attn_o = _attn_ctx(sink_l, q3, k3, v3)
            x1, h2t = _out_proj(pool_o.reshape(-1, PW), four_o.reshape(-1, FW), attn_o.reshape(-1, AW),
                                x, mod[l], ln2, w_out_l, seq, per_seq)
            a1, cnt, a2, r2 = _peer_scores(h2t, wqt_l, sk_l)
            x2 = _peer_dense(h2t, u_l, pv_l, a1, cnt, a2, r2, x1, mod[l], fn, seq, per_seq)
            return x2, k3, v3

        xp, k_l, v_l = layer(xp, B, S, False, None, tabs_p)
        xs, _, _ = layer(xs, BD, SD, True, rope, tabs_s)
        new_k.append(k_l.reshape(B, S, N_KV_HEADS, HEAD_DIM))
        new_v.append(v_l.reshape(B, S, N_KV_HEADS, HEAD_DIM))
    return (xp.reshape(B, S, D), xs.reshape(BD, SD, D), jnp.stack(new_k, axis=1), jnp.stack(new_v, axis=1))
```

```python
import functools
import math

import numpy as np
import jax
import jax.numpy as jnp
from jax import lax
from jax.experimental import pallas as pl
from jax.experimental.pallas import tpu as pltpu

F32 = jnp.float32
BF16 = jnp.bfloat16

HEAD_DIM = 64
N_KV_HEADS = 2
POOL_WINDOWS = (2, 4, 8, 16)
POOL_HALO = max(POOL_WINDOWS) // 2
FOURIER_GROUPS = 4
GRID_W = 64
WINDOW = 128
ATTN_BLOCK = 128
ROPE_THETA = 10000.0
ROPE_HALF = HEAD_DIM // 2
PEER_TOPK = 16
NORM_EPS = 1e-6
NEG_INF = -1e30
DFT_ROWS = 64
MOD_ROWS = 16
V7X_VMEM_LIMIT = 48 * 1024 * 1024
MXU_DEPTH = 256
PEER_TOKEN_TILE = 512
PEER_CHUNK_KEYS = 8
TIE_MARKS = 8
PEER_PIECES = 8
GELU_ROWS = 32


def _params(semantics):
    return pltpu.CompilerParams(dimension_semantics=semantics, vmem_limit_bytes=V7X_VMEM_LIMIT)


def _dot(a, b):
    return jnp.dot(a, b, preferred_element_type=F32)


def _dot_nt(a, b):
    return lax.dot_general(a, b, (((1,), (1,)), ((), ())), preferred_element_type=F32)


def _rms(x, g):
    return x * lax.rsqrt(jnp.mean(x * x, axis=-1, keepdims=True) + NORM_EPS) * g


def _ada_kernel(c_ref, w_ref, b_ref, o_ref):
    cv = c_ref[...]
    act = cv * (1.0 / (1.0 + jnp.exp(-cv)))
    o_ref[0] = _dot(act.astype(BF16), w_ref[0].astype(BF16)) + b_ref[0]


def _ada(cvec, w_ada, b_ada):
    L, D, N = w_ada.shape
    tn = N // 4
    return pl.pallas_call(
        _ada_kernel,
        grid=(L, N // tn),
        in_specs=[pl.BlockSpec((MOD_ROWS, D), lambda l, j: (0, 0)),
                  pl.BlockSpec((1, D, tn), lambda l, j: (l, 0, j)),
                  pl.BlockSpec((1, 1, tn), lambda l, j: (l, 0, j))],
        out_specs=pl.BlockSpec((1, MOD_ROWS, tn), lambda l, j: (l, 0, j)),
        out_shape=jax.ShapeDtypeStruct((L, MOD_ROWS, N), F32),
        compiler_params=_params(("parallel", "parallel")),
        name="ada_mod",
    )(cvec, w_ada, b_ada.reshape(L, 1, N))


def _rope_apply(x, cos, sin):
    lane = lax.broadcasted_iota(jnp.int32, (x.shape[0], 128), 1)
    first = (lane % ROPE_HALF) < (ROPE_HALF // 2)
    outs = []
    for g in range(x.shape[1] // 128):
        xg = x[:, g * 128:(g + 1) * 128]
        partner = jnp.where(first, pltpu.roll(xg, 128 - ROPE_HALF // 2, axis=1),
                            pltpu.roll(xg, ROPE_HALF // 2, axis=1))
        outs.append(xg * cos + partner * sin)
    return outs[0] if len(outs) == 1 else jnp.concatenate(outs, axis=1)


def _in_proj_kernel(*refs, D, PW, FW, AW, KW, rope):
    if rope:
        x_ref, mod_ref, ln_ref, w_ref, cs_ref, cos_ref, sin_ref, pool_ref, xcs_ref, q_ref, k_ref, v_ref = refs
    else:
        x_ref, mod_ref, ln_ref, w_ref, cs_ref, pool_ref, xcs_ref, q_ref, k_ref, v_ref = refs
    mod = mod_ref[0]
    sh1 = mod[:, 0:D]
    sc1 = mod[:, D:2 * D]
    h = _rms(x_ref[...], ln_ref[...]) * (1.0 + sc1) + sh1
    proj = _dot(h.astype(BF16), w_ref[...])
    o1, o2 = PW, PW + FW
    o3 = o2 + AW
    o4 = o3 + KW
    pool_ref[...] = proj[:, :o1]
    xcs_ref[...] = _dot(proj[:, o1:o2].astype(BF16), cs_ref[...]).astype(BF16)
    q = proj[:, o2:o3]
    k = proj[:, o3:o4]
    if rope:
        q = _rope_apply(q, cos_ref[...], sin_ref[...])
        k = _rope_apply(k, cos_ref[...], sin_ref[...])
    q_ref[...] = (q * (HEAD_DIM ** -0.5)).astype(BF16)
    k_ref[...] = k
    v_ref[...] = proj[:, o4:]


def _mod_spec(n_cols, tile, seq_len, per_seq):
    tiles_per_seq = max(seq_len // tile, 1)
    if per_seq:
        return pl.BlockSpec((1, 1, n_cols), lambda i, *_: (1 + i // tiles_per_seq, 0, 0))
    return pl.BlockSpec((1, 1, n_cols), lambda i, *_: (0, 0, 0))


def _in_proj(x, mod3, ln, w_in, cs, rope_tabs, seq_len, per_seq, dims):
    T, D = x.shape
    PW, FW, AW, KW = dims
    tm = min(256, seq_len)
    rope = rope_tabs is not None
    n_in = w_in.shape[1]
    tiles_per_seq = seq_len // tm
    in_specs = [pl.BlockSpec((tm, D), lambda i: (i, 0)),
                _mod_spec(6 * D, tm, seq_len, per_seq),
                pl.BlockSpec((1, D), lambda i: (0, 0)),
                pl.BlockSpec((D, n_in), lambda i: (0, 0)),
                pl.BlockSpec((FW, 2 * FW), lambda i: (0, 0))]
    args = [x, mod3, ln, w_in, cs]
    if rope:
        in_specs += [pl.BlockSpec((tm, 128), lambda i: (i % tiles_per_seq, 0))] * 2
        args += list(rope_tabs)
    outs = [(PW, F32), (2 * FW, BF16), (AW, BF16), (KW, F32), (KW, F32)]
    return pl.pallas_call(
        functools.partial(_in_proj_kernel, D=D, PW=PW, FW=FW, AW=AW, KW=KW, rope=rope),
        grid=(T // tm,),
        in_specs=in_specs,
        out_specs=[pl.BlockSpec((tm, w), lambda i: (i, 0)) for w, _ in outs],
        out_shape=[jax.ShapeDtypeStruct((T, w), dt) for w, dt in outs],
        compiler_params=_params(("parallel",)),
        name="in_proj_rope" if rope else "in_proj",
    )(*args)


def _pool_kernel(xp_ref, w_ref, sc_ref, o_ref, *, S, TP):
    c = pl.program_id(1)
    n = TP + 2 * POOL_HALO
    xs = xp_ref[0, pl.ds(pl.multiple_of(c * TP, 8), n), :]
    width = xs.shape[1]
    run = xs
    centred = []
    for w in POOL_WINDOWS:
        run = run + pltpu.roll(run, n - w // 2, axis=0)
        centred.append(pltpu.roll(run, w // 2, axis=0)[POOL_HALO:POOL_HALO + TP])
    lane = lax.broadcasted_iota(jnp.int32, (TP, width), 1)
    grp = lane // (width // len(POOL_WINDOWS))
    win = centred[-1]
    half = jnp.full((TP, width), POOL_WINDOWS[-1] // 2, jnp.int32)
    for gi in range(len(POOL_WINDOWS) - 2, -1, -1):
        win = jnp.where(grp == gi, centred[gi], win)
        half = jnp.where(grp == gi, POOL_WINDOWS[gi] // 2, half)
    t = c * TP + lax.broadcasted_iota(jnp.int32, (TP, width), 0)
    cnt = (jnp.clip(t + half, 0, S) - jnp.clip(t - half, 0, S)).astype(F32)
    y = win / cnt - xs[POOL_HALO:POOL_HALO + TP]
    o_ref[0] = (_dot(y.astype(BF16), w_ref[...]) * sc_ref[...]).astype(BF16)


def _pool(p_pool, w_bd, scale):
    B, S, W = p_pool.shape
    tp = min(256, S)
    xp = jnp.pad(p_pool, ((0, 0), (POOL_HALO, POOL_HALO), (0, 0)))
    return pl.pallas_call(
        functools.partial(_pool_kernel, S=S, TP=tp),
        grid=(B, S // tp),
        in_specs=[pl.BlockSpec((1, S + 2 * POOL_HALO, W), lambda b, c: (b, 0, 0)),
                  pl.BlockSpec((W, W), lambda b, c: (0, 0)),
                  pl.BlockSpec((1, W), lambda b, c: (0, 0))],
        out_specs=pl.BlockSpec((1, tp, W), lambda b, c: (b, c, 0)),
        out_shape=jax.ShapeDtypeStruct((B, S, W), BF16),
        compiler_params=_params(("parallel", "arbitrary")),
        name="pool_mixer",
    )(xp, w_bd, scale)


def _dft_kernel(ca_ref, sa_ref, cb_ref, sb_ref, c_ref, s_ref):
    ca, sa = ca_ref[0], sa_ref[0]
    cb, sb = cb_ref[...], sb_ref[...]
    c_ref[...] = (ca * cb - sa * sb).astype(BF16)
    s_ref[...] = (sa * cb + ca * sb).astype(BF16)


def _dft_tables(S):
    R = min(DFT_ROWS, S)
    k = np.arange(S, dtype=np.int64)
    a = np.arange(S // R, dtype=np.int64)[:, None]
    b = np.arange(R, dtype=np.int64)[:, None]
    ang_a = 2.0 * np.pi * ((a * R * k) % S) / S
    ang_b = 2.0 * np.pi * ((b * k) % S) / S
    ca = jnp.asarray(np.cos(ang_a), F32).reshape(S // R, 1, S)
    sa = jnp.asarray(np.sin(ang_a), F32).reshape(S // R, 1, S)
    cb = jnp.asarray(np.cos(ang_b), F32)
    sb = jnp.asarray(np.sin(ang_b), F32)
    row = pl.BlockSpec((1, 1, S), lambda i: (i, 0, 0))
    full = pl.BlockSpec((R, S), lambda i: (0, 0))
    out = pl.BlockSpec((R, S), lambda i: (i, 0))
    return pl.pallas_call(
        _dft_kernel,
        grid=(S // R,),
        in_specs=[row, row, full, full],
        out_specs=[out, out],
        out_shape=[jax.ShapeDtypeStruct((S, S), BF16)] * 2,
        compiler_params=_params(("parallel",)),
        name="dft_tables",
    )(ca, sa, cb, sb)


def _channel_dft(FW):
    gw = FW // FOURIER_GROUPS
    k = np.arange(gw)
    ang = 2.0 * np.pi * ((k[:, None] * k[None, :]) % gw) / gw
    cs = np.zeros((FW, 2 * FW), np.float64)
    for g in range(FOURIER_GROUPS):
        cs[g * gw:(g + 1) * gw, g * gw:(g + 1) * gw] = np.cos(ang)
        cs[g * gw:(g + 1) * gw, FW + g * gw:FW + (g + 1) * gw] = np.sin(ang)
    return jnp.asarray(cs, BF16)


def _four_kernel(c_ref, s_ref, xcs_ref, wf_ref, o_ref, *, FW, scale):
    y = _dot(c_ref[...], xcs_ref[0, :, :FW]) - _dot(s_ref[...], xcs_ref[0, :, FW:])
    o_ref[0] = _dot((y * scale).astype(BF16), wf_ref[...]).astype(BF16)


def _fourier(xcs, tabs, wf_bd):
    B, S, FW2 = xcs.shape
    FW = FW2 // 2
    tq = min(512, S)
    scale = 1.0 / math.sqrt(S * (FW // FOURIER_GROUPS))
    return pl.pallas_call(
        functools.partial(_four_kernel, FW=FW, scale=scale),
        grid=(S // tq, B),
        in_specs=[pl.BlockSpec((tq, S), lambda i, b: (i, 0)),
                  pl.BlockSpec((tq, S), lambda i, b: (i, 0)),
                  pl.BlockSpec((1, S, FW2), lambda i, b: (b, 0, 0)),
                  pl.BlockSpec((FW, FW), lambda i, b: (0, 0))],
        out_specs=pl.BlockSpec((1, tq, FW), lambda i, b: (b, i, 0)),
        out_shape=jax.ShapeDtypeStruct((B, S, FW), BF16),
        compiler_params=_params(("parallel", "arbitrary")),
        name="fourier_mixer",
    )(tabs[0], tabs[1], xcs, wf_bd)


def _softmax_pv(s, sink, v):
    m = jnp.maximum(jnp.max(s, axis=-1, keepdims=True), sink)
    p = jnp.exp(s - m)
    den = jnp.sum(p, axis=-1, keepdims=True) + jnp.exp(sink - m)
    return _dot(p.astype(BF16), v) / den


def _attn_ctx_kernel(sink_ref, q_ref, k_ref, v_ref, o_ref, *, G):
    q = q_ref[0]
    k = k_ref[0].astype(BF16)
    v = v_ref[0].astype(BF16)
    for kh in range(N_KV_HEADS):
        kk = k[:, kh * HEAD_DIM:(kh + 1) * HEAD_DIM]
        vv = v[:, kh * HEAD_DIM:(kh + 1) * HEAD_DIM]
        for g in range(G):
            hd = kh * G + g
            s = _dot_nt(q[:, hd * HEAD_DIM:(hd + 1) * HEAD_DIM], kk)
            o = _softmax_pv(s, sink_ref[hd], vv)
            o_ref[0, :, hd * HEAD_DIM:(hd + 1) * HEAD_DIM] = o.astype(BF16)


def _attn_ctx(sink, q, k, v):
    B, S, AW = q.shape
    KW = k.shape[2]
    G = AW // HEAD_DIM // N_KV_HEADS
    return pl.pallas_call(
        functools.partial(_attn_ctx_kernel, G=G),
        grid=(B,),
        in_specs=[pl.BlockSpec(memory_space=pltpu.SMEM),
                  pl.BlockSpec((1, S, AW), lambda b: (b, 0, 0)),
                  pl.BlockSpec((1, S, KW), lambda b: (b, 0, 0)),
                  pl.BlockSpec((1, S, KW), lambda b: (b, 0, 0))],
        out_specs=pl.BlockSpec((1, S, AW), lambda b: (b, 0, 0)),
        out_shape=jax.ShapeDtypeStruct((B, S, AW), BF16),
        compiler_params=_params(("parallel",)),
        name="attn_context",
    )(sink, q, k, v)


def _attn_lat_kernel(sink_ref, q_ref, kp_ref, kc_ref, kn_ref, vp_ref, vc_ref, vn_ref, ck_ref, cv_ref, o_ref,
                     *, S, G):
    i = pl.program_id(1)
    blk = ATTN_BLOCK
    q = q_ref[0]
    kcat = jnp.concatenate([kp_ref[0], kc_ref[0], kn_ref[0], ck_ref[0]], axis=0).astype(BF16)
    vcat = jnp.concatenate([vp_ref[0], vc_ref[0], vn_ref[0], cv_ref[0]], axis=0).astype(BF16)
    nk = kcat.shape[0]
    qi = lax.broadcasted_iota(jnp.int32, (blk, nk), 0)
    kj = lax.broadcasted_iota(jnp.int32, (blk, nk), 1)
    pos = i * blk - blk + kj
    local_ok = jnp.where(jnp.abs(kj - blk - qi) <= WINDOW,
                         jnp.where(pos >= 0, jnp.where(pos < S, 1, 0), 0), 0)
    ok = jnp.where(kj >= 3 * blk, 1, local_ok) > 0
    ok = jnp.concatenate([ok] * G, axis=0)
    for kh in range(N_KV_HEADS):
        kk = kcat[:, kh * HEAD_DIM:(kh + 1) * HEAD_DIM]
        vv = vcat[:, kh * HEAD_DIM:(kh + 1) * HEAD_DIM]
        qs = jnp.concatenate([q[:, (kh * G + g) * HEAD_DIM:(kh * G + g + 1) * HEAD_DIM] for g in range(G)], axis=0)
        sink = jnp.concatenate([jnp.full((blk, 1), sink_ref[kh * G + g], F32) for g in range(G)], axis=0)
        s = jnp.where(ok, _dot_nt(qs, kk), NEG_INF)
        o = _softmax_pv(s, sink, vv)
        for g in range(G):
            hd = kh * G + g
            o_ref[0, :, hd * HEAD_DIM:(hd + 1) * HEAD_DIM] = o[g * blk:(g + 1) * blk].astype(BF16)


def _attn_lat(sink, q, k, v, ck, cv):
    B, S, AW = q.shape
    KW = k.shape[2]
    C = ck.shape[1]
    G = AW // HEAD_DIM // N_KV_HEADS
    nb = S // ATTN_BLOCK
    prev = pl.BlockSpec((1, ATTN_BLOCK, KW), lambda b, i: (b, jnp.maximum(i - 1, 0), 0))
    cur = pl.BlockSpec((1, ATTN_BLOCK, KW), lambda b, i: (b, i, 0))
    nxt = pl.BlockSpec((1, ATTN_BLOCK, KW), lambda b, i: (b, jnp.minimum(i + 1, nb - 1), 0))
    ctx = pl.BlockSpec((1, C, KW), lambda b, i: (b, 0, 0))
    return pl.pallas_call(
        functools.partial(_attn_lat_kernel, S=S, G=G),
        grid=(B, nb),
        in_specs=[pl.BlockSpec(memory_space=pltpu.SMEM),
                  pl.BlockSpec((1, ATTN_BLOCK, AW), lambda b, i: (b, i, 0)),
                  prev, cur, nxt, prev, cur, nxt, ctx, ctx],
        out_specs=pl.BlockSpec((1, ATTN_BLOCK, AW), lambda b, i: (b, i, 0)),
        out_shape=jax.ShapeDtypeStruct((B, S, AW), BF16),
        compiler_params=_params(("parallel", "parallel")),
        name="attn_latent",
    )(sink, q, k, k, k, v, v, v, ck, cv)


def _out_proj_kernel(pool_ref, four_ref, attn_ref, x_ref, mod_ref, ln_ref, w_ref, x1_ref, h2t_ref, *, D, PW, FW):
    mixed = (_dot(pool_ref[...], w_ref[0:PW, :]) + _dot(four_ref[...], w_ref[PW:PW + FW, :])
             + _dot(attn_ref[...], w_ref[PW + FW:, :]))
    mod = mod_ref[0]
    g1 = mod[:, 2 * D:3 * D]
    sh2 = mod[:, 3 * D:4 * D]
    sc2 = mod[:, 4 * D:5 * D]
    x1 = x_ref[...] + g1 * mixed
    h2 = _rms(x1, ln_ref[...]) * (1.0 + sc2) + sh2
    x1_ref[...] = x1
    h2t_ref[...] = h2.T.astype(BF16)


def _out_proj(pool_o, four_o, attn_o, x, mod3, ln, w_out, seq_len, per_seq):
    T, D = x.shape
    PW, FW, AW = pool_o.shape[1], four_o.shape[1], attn_o.shape[1]
    tm = min(256, seq_len)
    return pl.pallas_call(
        functools.partial(_out_proj_kernel, D=D, PW=PW, FW=FW),
        grid=(T // tm,),
        in_specs=[pl.BlockSpec((tm, PW), lambda i: (i, 0)),
                  pl.BlockSpec((tm, FW), lambda i: (i, 0)),
                  pl.BlockSpec((tm, AW), lambda i: (i, 0)),
                  pl.BlockSpec((tm, D), lambda i: (i, 0)),
                  _mod_spec(6 * D, tm, seq_len, per_seq),
                  pl.BlockSpec((1, D), lambda i: (0, 0)),
                  pl.BlockSpec((PW + FW + AW, D), lambda i: (0, 0))],
        out_specs=[pl.BlockSpec((tm, D), lambda i: (i, 0)),
                   pl.BlockSpec((D, tm), lambda i: (0, i))],
        out_shape=[jax.ShapeDtypeStruct((T, D), F32), jax.ShapeDtypeStruct((D, T), BF16)],
        compiler_params=_params(("parallel",)),
        name="out_proj",
    )(pool_o, four_o, attn_o, x, mod3, ln, w_out)


def _top_values(vals, k, ranked=False):
    out = []
    rank = jnp.full(vals.shape, float(vals.shape[0]), F32) if ranked else None
    for r in range(k):
        m = jnp.max(vals, axis=0, keepdims=True)
        out.append(m)
        hit = vals == m
        if ranked:
            rank = jnp.where(hit, float(r), rank)
        if r + 1 < k:
            vals = jnp.where(hit, -jnp.inf, vals)
    return (out, rank) if ranked else out


def _sorting_network(n):
    def merge(lo, hi, r):
        step = r * 2
        if step < hi - lo:
            yield from merge(lo, hi, step)
            yield from merge(lo + r, hi, step)
            yield from [(i, i + r) for i in range(lo + r, hi - r, step)]
        else:
            yield (lo, lo + r)

    def sort(lo, hi):
        if hi - lo >= 1:
            mid = lo + (hi - lo) // 2
            yield from sort(lo, mid)
            yield from sort(mid + 1, hi)
            yield from merge(lo, hi, 1)

    return list(sort(0, n - 1))


def _top_values_sorted(vals, k):
    tiles = [vals[8 * i:8 * (i + 1)] for i in range(vals.shape[0] // 8)]
    for a, b in _sorting_network(len(tiles)):
        tiles[a], tiles[b] = jnp.maximum(tiles[a], tiles[b]), jnp.minimum(tiles[a], tiles[b])
    out = []
    for r in range(k):
        m = jnp.max(tiles[0], axis=0, keepdims=True)
        out.append(m)
        if r + 1 < k:
            hit = tiles[0] == m
            keep = min(len(tiles), k - r - 1)
            tiles = [jnp.where(hit, tiles[i + 1] if i + 1 < len(tiles) else -jnp.inf, tiles[i])
                     for i in range(keep)]
    return out


def _peer_scores_kernel(xt_ref, wqt_ref, sk_ref, a1_ref, cnt_ref, a2_ref, r2_ref, qt_ref, cand_ref, *, H, K):
    half = sk_ref.shape[3]
    qt_ref[...] = _dot(wqt_ref[...], xt_ref[...]).astype(BF16)
    pairs = [(i, j) for i in range(K) for j in range(K) if (i + 1) * (j + 1) <= K]
    cand_ref[...] = jnp.full(cand_ref.shape, -jnp.inf, F32)
    for h in range(H):
        s1 = _dot(sk_ref[h, 0], qt_ref[(2 * h) * half:(2 * h + 1) * half, :])
        s2 = _dot(sk_ref[h, 1], qt_ref[(2 * h + 1) * half:(2 * h + 2) * half, :])
        if s1.shape[0] == 8 * K and K & (K - 1) == 0:
            v1 = _top_values_sorted(s1, K)
            v2 = _top_values_sorted(s2, K)
            rank2 = jnp.full(s2.shape, float(s2.shape[0]), F32)
            for r in range(K):
                rank2 = jnp.where(s2 == v2[r], float(r), rank2)
        else:
            v1 = _top_values(s1, K)
            v2, rank2 = _top_values(s2, K, ranked=True)
        sums = {}
        for n, (i, j) in enumerate(pairs):
            sums[(i, j)] = v1[i] + v2[j]
            cand_ref[n:n + 1, :] = sums[(i, j)]
        best = _top_values_sorted(cand_ref[...], K)
        theta, top = best[K - 1], best[0]
        z = jnp.exp(best[0] - top)
        for r in range(1, K):
            z = z + jnp.exp(best[r] - top)
        cnt = jnp.zeros(s1.shape, F32)
        for i in range(K):
            ci = jnp.zeros(theta.shape, F32)
            for j in range(K):
                if (i, j) in sums:
                    ci = ci + jnp.where(sums[(i, j)] >= theta, 1.0, 0.0)
            cnt = jnp.where(s1 == v1[i], ci, cnt)
        a1_ref[h] = jnp.exp(s1 - v1[0]) * (1.0 / z)
        cnt_ref[h] = cnt
        a2 = jnp.exp(s2 - v2[0]).astype(BF16)
        rank2 = rank2.astype(BF16)
        for lc in range(s2.shape[1] // 128):
            ls = slice(lc * 128, (lc + 1) * 128)
            a2_ref[lc, h] = pltpu.bitcast(a2[:, ls], jnp.int32)
            r2_ref[lc, h] = pltpu.bitcast(rank2[:, ls], jnp.int32)


def _peer_scores(h2t, wqt, sk):
    D, T = h2t.shape
    H, _, NK, half = sk.shape
    tn = min(256, T)
    n_pairs = sum(1 for i in range(PEER_TOPK) for j in range(PEER_TOPK) if (i + 1) * (j + 1) <= PEER_TOPK)
    rows = pl.BlockSpec((H, NK, tn), lambda i: (0, 0, i))
    packed = pl.BlockSpec((tn // 128, H, NK // 2, 128), lambda i: (i, 0, 0, 0))
    return pl.pallas_call(
        functools.partial(_peer_scores_kernel, H=H, K=PEER_TOPK),
        grid=(T // tn,),
        in_specs=[pl.BlockSpec((D, tn), lambda i: (0, i)),
                  pl.BlockSpec(wqt.shape, lambda i: (0, 0)),
                  pl.BlockSpec(sk.shape, lambda i: (0, 0, 0, 0))],
        out_specs=[rows, rows, packed, packed],
        out_shape=[jax.ShapeDtypeStruct((H, NK, T), F32)] * 2 + [jax.ShapeDtypeStruct((T // 128, H, NK // 2, 128), jnp.int32)] * 2,
        scratch_shapes=[pltpu.VMEM((wqt.shape[0], tn), BF16),
                        pltpu.VMEM((8 * pl.next_power_of_2(-(-n_pairs // 8)), tn), F32)],
        compiler_params=_params(("parallel",)),
        name="peer_scores",
    )(h2t, wqt, sk)


def _gelu_tanh(x):
    return x * (0.5 * (1.0 + jnp.tanh(math.sqrt(2.0 / math.pi) * (x + 0.044715 * (x * x * x)))))


def _zero_of(v):
    u = pltpu.bitcast(v, jnp.uint32)
    return lax.shift_right_logical(lax.shift_right_logical(u, jnp.uint32(16)), jnp.uint32(16))


def _tied_rows(x, zeros):
    groups = x.shape[0] // 16
    parts = []
    for g in range(groups):
        blk = x[g * 16:(g + 1) * 16, :]
        zero = zeros[g * len(zeros) // groups]
        if zero is not None:
            head = pltpu.bitcast(blk[:, :MXU_DEPTH], jnp.uint32) + jnp.tile(zero, (1, MXU_DEPTH // 128))
            blk = jnp.concatenate([pltpu.bitcast(head, BF16), blk[:, MXU_DEPTH:]], axis=1)
        parts.append(blk)
    return jnp.concatenate(parts, axis=0)


def _peer_dense_kernel(*refs, D, H, NK, R, NJ, final):
    if final:
        (xt_ref, u_ref, vt_ref, a1_ref, cnt_ref, a2_ref, r2_ref, x1_ref, mod_ref, fn_ref, o_ref,
         acc_ref, act0_ref, act1_ref, p0_ref, p1_ref) = refs
    else:
        (xt_ref, u_ref, vt_ref, a1_ref, cnt_ref, a2_ref, r2_ref, x1_ref, mod_ref, o_ref,
         acc_ref, act0_ref, act1_ref, p0_ref, p1_ref) = refs
    s = pl.program_id(0)
    tn = xt_ref.shape[1]

    @pl.when(s == 0)
    def _():
        acc_ref[...] = jnp.zeros(acc_ref.shape, F32)
        act1_ref[...] = jnp.zeros(act1_ref.shape, BF16)
        p0_ref[...] = jnp.zeros(p0_ref.shape, BF16)

    ec = R * NK
    n_lc = tn // 128

    def gate_block(r, lc, act_b, p_b):
        ls = slice(lc * 128, (lc + 1) * 128)
        rows = slice(r * NK, (r + 1) * NK)
        gate = None
        marks = []
        for h in range(H):
            a1 = jnp.broadcast_to(a1_ref[h, r:r + 1, ls], (NK, 128)).astype(BF16)
            cnt = jnp.broadcast_to(cnt_ref[h, r:r + 1, ls], (NK, 128)).astype(BF16)
            rank2 = pltpu.bitcast(r2_ref[lc, h], BF16)
            a2 = pltpu.bitcast(a2_ref[lc, h], BF16)
            term = a1 * jnp.where(rank2 < cnt, a2, jnp.zeros((), BF16))
            gate = term if gate is None else gate + term
            if (h + 1) % (H // TIE_MARKS) == 0 and h + 1 < H:
                marks.append(_zero_of(gate[0:16, :]))
        p = gate * act_b[lc, rows, :]
        p_b[lc, rows, :] = p
        return marks + [_zero_of(p[0:16, :])]

    def stages(act_a, act_b, p_b, p_c):
        blocks = [(r, lc) for r in range(R) for lc in range(n_lc)]
        per_group = len(blocks) // (4 * PEER_PIECES)
        groups = [blocks[g * per_group:(g + 1) * per_group] for g in range(4 * PEER_PIECES)]

        def gate_group(g):
            marks = []
            for r, lc in groups[g]:
                marks += gate_block(r, lc, act_b, p_b)
            return marks

        xt = xt_ref[...]
        pc = jnp.concatenate([p_c[lc] for lc in range(n_lc)], axis=1)
        for q in range(PEER_PIECES):
            ra = slice(q * (ec // PEER_PIECES), (q + 1) * (ec // PEER_PIECES))
            marks = gate_group(4 * q) + gate_group(4 * q + 1)
            hq = _dot(_tied_rows(u_ref[ra, :], marks), xt)
            for r0 in range(ra.start, ra.stop, GELU_ROWS):
                for lc in range(n_lc):
                    ls = slice(lc * 128, (lc + 1) * 128)
                    act_a[lc, r0:r0 + GELU_ROWS, :] = _gelu_tanh(
                        hq[r0 - ra.start:r0 - ra.start + GELU_ROWS, ls].astype(BF16))
            rc = slice(q * (D // PEER_PIECES), (q + 1) * (D // PEER_PIECES))
            marks = gate_group(4 * q + 2) + gate_group(4 * q + 3)
            acc_ref[rc, :] += _dot(_tied_rows(vt_ref[0, rc, :], marks), pc)

    @pl.when(s % 2 == 0)
    def _():
        stages(act0_ref, act1_ref, p1_ref, p0_ref)

    @pl.when(s % 2 == 1)
    def _():
        stages(act1_ref, act0_ref, p0_ref, p1_ref)

    c = s - 2

    @pl.when(jnp.logical_and(c >= 0, c % NJ == NJ - 1))
    def _():
        g2 = mod_ref[0][:, 5 * D:6 * D]
        x2 = x1_ref[...] + g2 * acc_ref[...].T
        o_ref[...] = _rms(x2, fn_ref[...]) if final else x2
        acc_ref[...] = jnp.zeros(acc_ref.shape, F32)


def _peer_dense(h2t, u, vt, a1, cnt, a2, r2, x1, mod3, final_norm, seq_len, per_seq):
    D, T = h2t.shape
    E = u.shape[0]
    H, NK, _ = a1.shape
    R = PEER_CHUNK_KEYS
    ec = R * NK
    vt = vt.reshape(E // ec, ec, D).transpose(0, 2, 1)
    tn = min(PEER_TOKEN_TILE, seq_len if per_seq else T)
    final = final_norm is not None
    nj = E // ec
    total = (T // tn) * nj
    tiles_per_seq = max(seq_len // tn, 1)

    def item(lag):
        return lambda s: jnp.clip(s - lag, 0, total - 1)

    ia, ib, ic = item(0), item(1), item(2)
    mod_row = (lambda s: 1 + (ic(s) // nj) // tiles_per_seq) if per_seq else (lambda s: 0)
    tab = pl.BlockSpec((tn // 128, H, NK // 2, 128), lambda s: (ib(s) // nj, 0, 0, 0))
    row = pl.BlockSpec((H, R, tn), lambda s: (0, ib(s) % nj, ib(s) // nj))
    in_specs = [pl.BlockSpec((D, tn), lambda s: (0, ia(s) // nj)),
                pl.BlockSpec((ec, D), lambda s: (ia(s) % nj, 0)),
                pl.BlockSpec((1, D, ec), lambda s: (ic(s) % nj, 0, 0)),
                row, row, tab, tab,
                pl.BlockSpec((tn, D), lambda s: (ic(s) // nj, 0)),
                pl.BlockSpec((1, 1, 6 * D), lambda s: (mod_row(s), 0, 0))]
    args = [h2t, u, vt, a1, cnt, a2, r2, x1, mod3]
    if final:
        in_specs.append(pl.BlockSpec((1, D), lambda s: (0, 0)))
        args.append(final_norm)
    return pl.pallas_call(
        functools.partial(_peer_dense_kernel, D=D, H=H, NK=NK, R=R, NJ=nj, final=final),
        grid=(total + 2,),
        in_specs=in_specs,
        out_specs=pl.BlockSpec((tn, D), lambda s: (ic(s) // nj, 0)),
        out_shape=jax.ShapeDtypeStruct((T, D), F32),
        scratch_shapes=[pltpu.VMEM((D, tn), F32)] + [pltpu.VMEM((tn // 128, ec, 128), BF16)] * 4,
        compiler_params=_params(("arbitrary",)),
        name="peer_dense_final" if final else "peer_dense",
    )(*args)


def _rope_tables(S):
    pos = np.arange(S)
    inv_freq = ROPE_THETA ** (-(np.arange(ROPE_HALF // 2, dtype=np.float64) * 2.0 / ROPE_HALF))
    d = np.arange(HEAD_DIM)
    p = np.where(d[None, :] < ROPE_HALF, (pos // GRID_W)[:, None], (pos % GRID_W)[:, None])
    ang = p * inv_freq[d % (ROPE_HALF // 2)][None, :]
    sign = np.where((d % ROPE_HALF) < ROPE_HALF // 2, -1.0, 1.0)[None, :]
    cos = np.tile(np.cos(ang), (1, 128 // HEAD_DIM))
    sin = np.tile(np.sin(ang) * sign, (1, 128 // HEAD_DIM))
    return jnp.asarray(cos, F32), jnp.asarray(sin, F32)


def _block_diag(w):
    g, a, b = w.shape
    out = jnp.zeros((g * a, g * b), w.dtype)
    for i in range(g):
        out = out.at[i * a:(i + 1) * a, i * b:(i + 1) * b].set(w[i])
    return out


def kernel(x_prompt, x_sample, cache_k, cache_v, c, c_ctx, w_ada, b_ada, norm_mix, norm_ffn, w_in, w_pool,
           pool_scale, w_fourier, attn_sink, w_out, peer_w_query, peer_sub_keys, peer_u, peer_v, final_norm):
    B, S, D = x_prompt.shape
    BD, SD, _ = x_sample.shape
    L = w_ada.shape[0]
    C = cache_k.shape[2]
    PW = w_pool.shape[1] * w_pool.shape[2]
    FW = w_fourier.shape[1] * w_fourier.shape[2]
    KW = cache_k.shape[3] * cache_k.shape[4]
    AW = w_in.shape[2] - PW - FW - 2 * KW
    dims = (PW, FW, AW, KW)
    assert BD + 1 <= MOD_ROWS and SD % ATTN_BLOCK == 0 and SD % GRID_W == 0

    cvec = jnp.concatenate([c_ctx[None, :], c, jnp.zeros((MOD_ROWS - 1 - BD, D), F32)], axis=0)
    mod = _ada(cvec, w_ada, b_ada).reshape(L, MOD_ROWS, 1, 6 * D)
    cs = _channel_dft(FW)
    tabs_p = _dft_tables(S)
    tabs_s = _dft_tables(SD)
    rope = _rope_tables(SD)

    xp = x_prompt.reshape(B * S, D)
    xs = x_sample.reshape(BD * SD, D)
    new_k, new_v = [], []
    for l in range(L):
        last = l == L - 1
        w_in_l = w_in[l].astype(BF16)
        w_pool_l = _block_diag(w_pool[l]).astype(BF16)
        w_four_l = _block_diag(w_fourier[l]).astype(BF16)
        w_out_l = w_out[l].astype(BF16)
        wqt_l = peer_w_query[l].T.astype(BF16)
        sk_l = peer_sub_keys[l].astype(BF16)
        u_l = peer_u[l].astype(BF16)
        pv_l = peer_v[l].astype(BF16)
        ln1 = norm_mix[l][None, :]
        ln2 = norm_ffn[l][None, :]
        scale_l = pool_scale[l][None, :]
        sink_l = attn_sink[l]
        fn = final_norm[None, :] if last else None
        ck = cache_k[:, l].reshape(BD, C, KW)
        cv = cache_v[:, l].reshape(BD, C, KW)

        def layer(x, nb, seq, per_seq, rope_tabs, tabs):
            pool_in, xcs, q, k, v = _in_proj(x, mod[l], ln1, w_in_l, cs, rope_tabs, seq, per_seq, dims)
            pool_o = _pool(pool_in.reshape(nb, seq, PW), w_pool_l, scale_l)
            four_o = _fourier(xcs.reshape(nb, seq, 2 * FW), tabs, w_four_l)
            q3, k3, v3 = q.reshape(nb, seq, AW), k.reshape(nb, seq, KW), v.reshape(nb, seq, KW)
            if per_seq:
                attn_o = _attn_lat(sink_l, q3, k3, v3, ck, cv)
            else:
                attn_o = _attn_ctx(sink_l, q3, k3, v3)
            x1, h2t = _out_proj(pool_o.reshape(-1, PW), four_o.reshape(-1, FW), attn_o.reshape(-1, AW),
                                x, mod[l], ln2, w_out_l, seq, per_seq)
            a1, cnt, a2, r2 = _peer_scores(h2t, wqt_l, sk_l)
            x2 = _peer_dense(h2t, u_l, pv_l, a1, cnt, a2, r2, x1, mod[l], fn, seq, per_seq)
            return x2, k3, v3

        xp, k_l, v_l = layer(xp, B, S, False, None, tabs_p)
        xs, _, _ = layer(xs, BD, SD, True, rope, tabs_s)
        new_k.append(k_l.reshape(B, S, N_KV_HEADS, HEAD_DIM))
        new_v.append(v_l.reshape(B, S, N_KV_HEADS, HEAD_DIM))
    return (xp.reshape(B, S, D), xs.reshape(BD, SD, D), jnp.stack(new_k, axis=1), jnp.stack(new_v, axis=1))
```
